```python
import jax, jax.numpy as jnp
from jax import lax
import numpy as np

D_MODEL = 1024
BATCH = 8
SEQ = 2048
DEPTH = 2

HEAD_DIM = 64
RET_HEADS = 4
RET_W = RET_HEADS * HEAD_DIM
RET_CHUNK = 128
ROPE_BASE = 10000.0
MOBA_HEADS = 6
MOBA_W = MOBA_HEADS * HEAD_DIM
MOBA_BLOCK = 256
MOBA_TOPK = 3
MOBA_QCHUNK = 32
LRU_BLOCKS = 6
LRU_BLOCK_W = 64
LRU_W = LRU_BLOCKS * LRU_BLOCK_W
CONV_WIDTH = 4
LRU_C = 8.0
MIX_W = RET_W + MOBA_W + LRU_W
IN_SIZES = [RET_W] * 4 + [MOBA_W] * 3 + [LRU_W] * 2
IN_W = sum(IN_SIZES)
IN_OFFSETS = np.cumsum(IN_SIZES)[:-1].tolist()
N_EXPERTS = 32
TOPK_EXPERTS = 4
D_FF = D_MODEL
SWIGLU_LIMIT = 7.0
SWIGLU_ALPHA = 1.702
EXPERT_BLOCK = 256
NORM_EPS = 1e-6

kernel_name = "hybrid_retention_moba_rglru_moe_adaln"


def rms_norm(x, w):
    xf = x.astype(jnp.float32)
    y = xf * lax.rsqrt(jnp.mean(xf * xf, axis=-1, keepdims=True) + NORM_EPS)
    return (y * w.astype(jnp.float32)).astype(x.dtype)


def rotary(x, pos):
    half = x.shape[-1] // 2
    inv = 1.0 / (ROPE_BASE ** (jnp.arange(half, dtype=jnp.float32) / half))
    ang = pos.astype(jnp.float32)[:, None] * inv[None, :]
    cos = jnp.cos(ang)[None, :, None, :]
    sin = jnp.sin(ang)[None, :, None, :]
    xf = x.astype(jnp.float32)
    x1, x2 = xf[..., :half], xf[..., half:]
    return jnp.concatenate([x1 * cos - x2 * sin, x2 * cos + x1 * sin], axis=-1)


def retention(q, k, v, g, norm_w):
    B, S, _ = q.shape
    H, dh, C = RET_HEADS, HEAD_DIM, RET_CHUNK
    N = S // C
    pos = jnp.arange(S)
    qh = rotary(q.reshape(B, S, H, dh), pos)
    kh = rotary(k.reshape(B, S, H, dh), pos) * (dh ** -0.5)
    vh = v.reshape(B, S, H, dh).astype(jnp.float32)
    to_chunks = lambda t: t.reshape(B, N, C, H, dh).transpose(0, 3, 1, 2, 4)
    qc, kc, vc = to_chunks(qh), to_chunks(kh), to_chunks(vh)
    log_g = jnp.log1p(-jnp.exp2(-5.0 - jnp.arange(H, dtype=jnp.float32)))
    idx = jnp.arange(C, dtype=jnp.float32)
    diff = idx[:, None] - idx[None, :]
    dmat = jnp.where(diff >= 0, jnp.exp(log_g[:, None, None] * jnp.maximum(diff, 0.0)), 0.0)
    scores = jnp.einsum('bhnid,bhnjd->bhnij', qc, kc) * dmat[None, :, None]
    intra = jnp.einsum('bhnij,bhnje->bhnie', scores, vc)
    zeta = jnp.exp(log_g[:, None] * (C - 1 - idx)[None, :])
    xi = jnp.exp(log_g[:, None] * (idx + 1)[None, :])
    kv = jnp.einsum('bhnjd,bhnje->nbhde', kc * zeta[None, :, None, :, None], vc)
    chunk_decay = jnp.exp(log_g * C)[None, :, None, None]

    def step(state, kv_n):
        return state * chunk_decay + kv_n, state

    _, prev = lax.scan(step, jnp.zeros((B, H, dh, dh), jnp.float32), kv)
    cross = jnp.einsum('bhnid,nbhde->bhnie', qc, prev) * xi[None, :, None, :, None]
    y = (intra + cross).transpose(0, 2, 3, 1, 4).reshape(B, S, H, dh)
    mu = jnp.mean(y, axis=-1, keepdims=True)
    var = jnp.mean(jnp.square(y - mu), axis=-1, keepdims=True)
    y = ((y - mu) * lax.rsqrt(var + NORM_EPS)).reshape(B, S, RET_W) * norm_w.astype(jnp.float32)
    return (jax.nn.silu(g.astype(jnp.float32)) * y).astype(q.dtype)


def moba_attention(q, k, v):
    B, S, _ = q.shape
    H, dh, Bk, Qc = MOBA_HEADS, HEAD_DIM, MOBA_BLOCK, MOBA_QCHUNK
    NBk = -(-S // Bk)
    Sp = NBk * Bk
    topk = min(MOBA_TOPK, NBk)
    to_heads = lambda t: t.reshape(B, S, H, dh).transpose(0, 2, 1, 3).astype(jnp.float32)
    qh, kh, vh = to_heads(q), to_heads(k), to_heads(v)
    pad = ((0, 0), (0, 0), (0, Sp - S), (0, 0))
    kb = jnp.pad(kh, pad).reshape(B, H, NBk, Bk, dh)
    vb = jnp.pad(vh, pad).reshape(B, H, NBk, Bk, dh)
    kmean = jnp.mean(kb, axis=3)
    NQ = S // Qc
    qch = qh.reshape(B, H, NQ, Qc, dh).transpose(2, 0, 1, 3, 4)
    bi = jnp.arange(B)[:, None, None, None]
    hi = jnp.arange(H)[None, :, None, None]
    scale = dh ** -0.5
    blk_ids = jnp.arange(NBk)

    def chunk(args):
        qq, ci = args
        qpos = ci * Qc + jnp.arange(Qc)
        own = (ci * Qc) // Bk
        gs = jnp.einsum('bhqd,bhnd->bhqn', qq, kmean)
        gs = jnp.where(blk_ids < own, gs, -jnp.inf)
        _, sel = lax.top_k(gs, topk)
        valid = sel < own
        kg = kb[bi, hi, sel]
        vg = vb[bi, hi, sel]
        s_sel = jnp.einsum('bhqd,bhqnkd->bhqnk', qq, kg) * scale
        s_sel = jnp.where(valid[..., None], s_sel, -jnp.inf).reshape(B, H, Qc, topk * Bk)
        ko = lax.dynamic_index_in_dim(kb, own, axis=2, keepdims=False)
        vo = lax.dynamic_index_in_dim(vb, own, axis=2, keepdims=False)
        kpos = own * Bk + jnp.arange(Bk)
        s_own = jnp.einsum('bhqd,bhkd->bhqk', qq, ko) * scale
        s_own = jnp.where(kpos[None, :] <= qpos[:, None], s_own, -jnp.inf)
        p = jax.nn.softmax(jnp.concatenate([s_sel, s_own], axis=-1), axis=-1)
        p_sel = p[..., :topk * Bk].reshape(B, H, Qc, topk, Bk)
        p_own = p[..., topk * Bk:]
        return (jnp.einsum('bhqnk,bhqnkd->bhqd', p_sel, vg)
                + jnp.einsum('bhqk,bhkd->bhqd', p_own, vo))

    out = lax.map(chunk, (qch, jnp.arange(NQ)))
    return out.transpose(1, 0, 3, 2, 4).reshape(B, S, MOBA_W).astype(q.dtype)


def rg_lru_branch(xb, gb, conv_w, conv_b, wa, ba, wx, bx, lam):
    B, S, _ = xb.shape
    u = lax.conv_general_dilated(xb, conv_w[:, None, :], window_strides=(1,),
                                 padding=[(CONV_WIDTH - 1, 0)],
                                 dimension_numbers=('NWC', 'WIO', 'NWC'),
                                 feature_group_count=LRU_W) + conv_b
    uf = u.astype(jnp.float32)
    ub = uf.reshape(B, S, LRU_BLOCKS, LRU_BLOCK_W)
    r = jax.nn.sigmoid(jnp.einsum('bsgi,gij->bsgj', ub, wa.astype(jnp.float32)).reshape(B, S, LRU_W) + ba)
    i = jax.nn.sigmoid(jnp.einsum('bsgi,gij->bsgj', ub, wx.astype(jnp.float32)).reshape(B, S, LRU_W) + bx)
    log_a = -LRU_C * r * jax.nn.softplus(-lam.astype(jnp.float32))
    a = jnp.exp(log_a)
    b = jnp.sqrt(-jnp.expm1(2.0 * log_a)) * (i * uf)

    def combine(left, right):
        a1, b1 = left
        a2, b2 = right
        return a1 * a2, a2 * b1 + b2

    _, h = lax.associative_scan(combine, (a, b), axis=1)
    return (jax.nn.gelu(gb.astype(jnp.float32)) * h).astype(xb.dtype)


def moe_ffn(h, router_w, router_b, w_gu, b_gu, w_dn, b_dn):
    B, S, D = h.shape
    T = B * S
    ht = h.reshape(T, D)
    logits = (ht @ router_w + router_b).astype(jnp.float32)
    top_v, top_i = lax.top_k(logits, TOPK_EXPERTS)
    gates = jax.nn.softmax(top_v, axis=-1)
    A = T * TOPK_EXPERTS
    e_flat = top_i.reshape(A)
    tok_flat = jnp.repeat(jnp.arange(T), TOPK_EXPERTS)
    g_flat = gates.reshape(A)
    order = jnp.argsort(e_flat, stable=True)
    e_sorted, tok_sorted, g_sorted = e_flat[order], tok_flat[order], g_flat[order]
    counts = jnp.bincount(e_flat, length=N_EXPERTS)
    padded = ((counts + EXPERT_BLOCK - 1) // EXPERT_BLOCK) * EXPERT_BLOCK
    pend = jnp.cumsum(padded)
    pstart = pend - padded
    ustart = jnp.cumsum(counts) - counts
    ppos = pstart[e_sorted] + (jnp.arange(A) - ustart[e_sorted])
    NB = -(-A // EXPERT_BLOCK) + N_EXPERTS
    P = NB * EXPERT_BLOCK
    row_tok = jnp.zeros((P,), jnp.int32).at[ppos].set(tok_sorted.astype(jnp.int32))
    row_g = jnp.zeros((P,), jnp.float32).at[ppos].set(g_sorted)
    blk_e = jnp.minimum(jnp.searchsorted(pend, jnp.arange(NB) * EXPERT_BLOCK, side='right'), N_EXPERTS - 1)
    xr = ht[row_tok].reshape(NB, EXPERT_BLOCK, D)

    def expert_block(args):
        xb, e = args
        gu = (xb @ w_gu[e] + b_gu[e]).astype(jnp.float32)
        gate, up = gu[:, :D_FF], gu[:, D_FF:]
        gate = jnp.minimum(gate, SWIGLU_LIMIT)
        up = jnp.clip(up, -SWIGLU_LIMIT, SWIGLU_LIMIT)
        act = ((up + 1.0) * gate * jax.nn.sigmoid(SWIGLU_ALPHA * gate)).astype(xb.dtype)
        return act @ w_dn[e] + b_dn[e]

    yr = lax.map(expert_block, (xr, blk_e)).reshape(P, D)
    y = jnp.zeros((T, D), jnp.float32).at[row_tok].add(yr.astype(jnp.float32) * row_g[:, None])
    return y.reshape(B, S, D).astype(h.dtype)


def setup_inputs(seed: int = 0) -> dict:
    key = jax.random.key(seed)
    ks = jax.random.split(key, 24)
    f32 = jnp.float32
    nrm = lambda k, shape, s: jax.random.normal(k, shape, f32) * s
    u = jax.random.uniform(ks[12], (DEPTH, LRU_W), f32, 0.9, 0.999)
    a_base = u ** (1.0 / LRU_C)
    return {
        "x": nrm(ks[0], (BATCH, SEQ, D_MODEL), 1.0),
        "c": nrm(ks[1], (BATCH, D_MODEL), 1.0),
        "ada_w": nrm(ks[2], (DEPTH, D_MODEL, 6 * D_MODEL), 0.5 * D_MODEL ** -0.5),
        "ada_b": nrm(ks[3], (DEPTH, 6 * D_MODEL), 0.02),
        "norm_mix_w": 1.0 + nrm(ks[4], (DEPTH, D_MODEL), 0.02),
        "w_in": nrm(ks[5], (DEPTH, D_MODEL, IN_W), D_MODEL ** -0.5),
        "ret_norm_w": 1.0 + nrm(ks[6], (DEPTH, RET_W), 0.02),
        "lru_conv_w": nrm(ks[7], (DEPTH, CONV_WIDTH, LRU_W), CONV_WIDTH ** -0.5),
        "lru_conv_b": nrm(ks[8], (DEPTH, LRU_W), 0.02),
        "lru_gate_a_w": nrm(ks[9], (DEPTH, LRU_BLOCKS, LRU_BLOCK_W, LRU_BLOCK_W), LRU_BLOCK_W ** -0.5),
        "lru_gate_a_b": nrm(ks[10], (DEPTH, LRU_W), 0.02),
        "lru_gate_x_w": nrm(ks[11], (DEPTH, LRU_BLOCKS, LRU_BLOCK_W, LRU_BLOCK_W), LRU_BLOCK_W ** -0.5),
        "lru_gate_x_b": nrm(ks[13], (DEPTH, LRU_W), 0.02),
        "lru_lambda": jnp.log(a_base) - jnp.log1p(-a_base),
        "w_out": nrm(ks[14], (DEPTH, MIX_W, D_MODEL), MIX_W ** -0.5),
        "norm_ffn_w": 1.0 + nrm(ks[15], (DEPTH, D_MODEL), 0.02),
        "router_w": nrm(ks[16], (DEPTH, D_MODEL, N_EXPERTS), D_MODEL ** -0.5),
        "router_b": nrm(ks[17], (DEPTH, N_EXPERTS), 0.01),
        "moe_w_gu": nrm(ks[18], (DEPTH, N_EXPERTS, D_MODEL, 2 * D_FF), D_MODEL ** -0.5),
        "moe_b_gu": nrm(ks[19], (DEPTH, N_EXPERTS, 2 * D_FF), 0.02),
        "moe_w_down": nrm(ks[20], (DEPTH, N_EXPERTS, D_FF, D_MODEL), D_FF ** -0.5),
        "moe_b_down": nrm(ks[21], (DEPTH, N_EXPERTS, D_MODEL), 0.02),
        "final_norm_w": 1.0 + nrm(ks[22], (D_MODEL,), 0.02),
    }


def reference(x, c, ada_w, ada_b, norm_mix_w, w_in, ret_norm_w, lru_conv_w, lru_conv_b,
              lru_gate_a_w, lru_gate_a_b, lru_gate_x_w, lru_gate_x_b, lru_lambda, w_out,
              norm_ffn_w, router_w, router_b, moe_w_gu, moe_b_gu, moe_w_down, moe_b_down,
              final_norm_w):
    c_act = jax.nn.silu(c)
    for l in range(DEPTH):
        mod = c_act @ ada_w[l] + ada_b[l]
        shift1, scale1, gate1, shift2, scale2, gate2 = [m[:, None, :] for m in jnp.split(mod, 6, axis=-1)]
        h = rms_norm(x, norm_mix_w[l]) * (1.0 + scale1) + shift1
        proj = h @ w_in[l]
        rq, rk, rv, rg, mq, mk, mv, lx, lg = jnp.split(proj, IN_OFFSETS, axis=-1)
        y_ret = retention(rq, rk, rv, rg, ret_norm_w[l])
        y_moba = moba_attention(mq, mk, mv)
        y_lru = rg_lru_branch(lx, lg, lru_conv_w[l], lru_conv_b[l], lru_gate_a_w[l], lru_gate_a_b[l],
                              lru_gate_x_w[l], lru_gate_x_b[l], lru_lambda[l])
        mixed = jnp.concatenate([y_ret, y_moba, y_lru], axis=-1) @ w_out[l]
        x = x + gate1 * mixed
        h2 = rms_norm(x, norm_ffn_w[l]) * (1.0 + scale2) + shift2
        x = x + gate2 * moe_ffn(h2, router_w[l], router_b[l], moe_w_gu[l], moe_b_gu[l],
                                moe_w_down[l], moe_b_down[l])
    return rms_norm(x, final_norm_w)
```

```python
import functools

import jax
import jax.numpy as jnp
from jax import lax
from jax.experimental import pallas as pl
from jax.experimental.pallas import tpu as pltpu

F32 = jnp.float32
BF16 = jnp.bfloat16

HEAD_DIM = 64
RET_HEADS = 4
RET_W = RET_HEADS * HEAD_DIM
RET_CHUNK = 128
ROPE_BASE = 10000.0
MOBA_HEADS = 6
MOBA_W = MOBA_HEADS * HEAD_DIM
MOBA_BLOCK = 256
MOBA_TOPK = 3
LRU_BLOCKS = 6
LRU_BLOCK_W = 64
LRU_W = LRU_BLOCKS * LRU_BLOCK_W
CONV_WIDTH = 4
LRU_C = 8.0
N_EXPERTS = 32
TOPK_EXPERTS = 4
SWIGLU_LIMIT = 7.0
SWIGLU_ALPHA = 1.702
NORM_EPS = 1e-6

LANES = 128
VMEM_LIMIT = 56 * 1024 * 1024
ROW_TILE = 512
EXPERT_ROWS = 256
COMBINE_ROWS = 256

NT_DIMS = (((1,), (1,)), ((), ()))
NN_DIMS = (((1,), (0,)), ((), ()))
TN_DIMS = (((0,), (0,)), ((), ()))


def _params(*semantics):
    return pltpu.CompilerParams(dimension_semantics=semantics, vmem_limit_bytes=VMEM_LIMIT)


def _dot(a, b, dims=NN_DIMS):
    return lax.dot_general(a, b, dims, preferred_element_type=F32)


def _split_bf16(a):
    hi = a.astype(BF16)
    lo = (a - hi.astype(F32)).astype(BF16)
    return hi, lo


def _dot3(a, b, dims=NN_DIMS):
    ah, al = _split_bf16(a)
    bh, bl = _split_bf16(b)
    return _dot(ah, bl, dims) + _dot(al, bh, dims) + _dot(ah, bh, dims)


def _rms_mod(x, nw, scale, shift):
    ms = jnp.mean(x * x, axis=-1, keepdims=True)
    y = x * lax.rsqrt(ms + NORM_EPS) * nw
    return y * (1.0 + scale) + shift


def _ada_kernel(c_ref, w_ref, b_ref, o_ref):
    c = c_ref[...]
    c_act = c * jax.nn.sigmoid(c)
    o_ref[0] = _dot3(c_act, w_ref[0]) + b_ref[0]


def _ada_mod(c, ada_w, ada_b):
    depth, d, n = ada_w.shape
    b = c.shape[0]
    tn = n // 4
    return pl.pallas_call(
        _ada_kernel,
        grid=(depth, n // tn),
        in_specs=[
            pl.BlockSpec((b, d), lambda l, j: (0, 0)),
            pl.BlockSpec((1, d, tn), lambda l, j: (l, 0, j)),
            pl.BlockSpec((1, 1, tn), lambda l, j: (l, 0, j)),
        ],
        out_specs=pl.BlockSpec((1, b, tn), lambda l, j: (l, 0, j)),
        out_shape=jax.ShapeDtypeStruct((depth, b, n), F32),
        compiler_params=_params("arbitrary", "arbitrary"),
        name="ada_mod",
    )(c, ada_w, ada_b.reshape(depth, 1, n))


def _inproj_kernel(x_ref, nw_ref, sc_ref, sh_ref, w_ref, ret_ref, moba_ref, lru_ref):
    h = _rms_mod(x_ref[...], nw_ref[...], sc_ref[0], sh_ref[0]).astype(BF16)
    o0 = RET_W * 4
    o1 = o0 + MOBA_W * 3
    ret_ref[...] = _dot(h, w_ref[:, 0:o0])
    moba_ref[...] = _dot(h, w_ref[:, o0:o1])
    lru_ref[...] = _dot(h, w_ref[:, o1:])


def _inproj(x2d, nw, scale, shift, w_bf16, seq):
    t, d = x2d.shape
    tm = ROW_TILE
    per_b = seq // tm
    n_ret, n_moba, n_lru = RET_W * 4, MOBA_W * 3, LRU_W * 2
    bmap = lambda i: (i // per_b, 0, 0)
    return pl.pallas_call(
        _inproj_kernel,
        grid=(t // tm,),
        in_specs=[
            pl.BlockSpec((tm, d), lambda i: (i, 0)),
            pl.BlockSpec((1, d), lambda i: (0, 0)),
            pl.BlockSpec((1, 1, d), bmap),
            pl.BlockSpec((1, 1, d), bmap),
            pl.BlockSpec(w_bf16.shape, lambda i: (0, 0)),
        ],
        out_specs=[
            pl.BlockSpec((tm, n_ret), lambda i: (i, 0)),
            pl.BlockSpec((tm, n_moba), lambda i: (i, 0)),
            pl.BlockSpec((tm, n_lru), lambda i: (i, 0)),
        ],
        out_shape=[
            jax.ShapeDtypeStruct((t, n_ret), F32),
            jax.ShapeDtypeStruct((t, n_moba), F32),
            jax.ShapeDtypeStruct((t, n_lru), F32),
        ],
        compiler_params=_params("arbitrary"),
        name="inproj",
    )(x2d, nw, scale, shift, w_bf16)


def _retention_tables(seq):
    h, dh, c = RET_HEADS, HEAD_DIM, RET_CHUNK
    half = dh // 2
    inv = 1.0 / (ROPE_BASE ** (jnp.arange(half, dtype=F32) / half))
    ang = jnp.arange(seq).astype(F32)[:, None] * inv[None, :]
    cos, sin = jnp.cos(ang), jnp.sin(ang)
    cos_t = jnp.tile(jnp.concatenate([cos, cos], axis=-1), (1, h))
    sin_t = jnp.tile(jnp.concatenate([-sin, sin], axis=-1), (1, h))
    log_g = jnp.log1p(-jnp.exp2(-5.0 - jnp.arange(h, dtype=F32)))
    idx = jnp.arange(c, dtype=F32)
    diff = idx[:, None] - idx[None, :]
    dmat = jnp.where(diff >= 0, jnp.exp(log_g[:, None, None] * jnp.maximum(diff, 0.0)), 0.0)
    zeta = jnp.exp(log_g[:, None] * (c - 1 - idx)[None, :])
    xi = jnp.exp(log_g[:, None] * (idx + 1)[None, :])
    zeta_t = jnp.repeat(zeta.T, dh, axis=1)
    xi_t = jnp.repeat(xi.T, dh, axis=1)
    decay = jnp.broadcast_to(jnp.exp(log_g * c)[:, None, None], (h, dh, dh))
    return cos_t, sin_t, dmat, zeta_t, xi_t, decay


def _retention_kernel(p_ref, cos_ref, sin_ref, dmat_ref, zeta_ref, xi_ref, dec_ref, nw_ref, o_ref):
    seq = p_ref.shape[0]
    c, dh, w = RET_CHUNK, HEAD_DIM, RET_W
    lane = lax.broadcasted_iota(jnp.int32, (1, w), 1)
    first_half = (lane % dh) < (dh // 2)

    def rope(x, cos, sin):
        partner = jnp.where(first_half, pltpu.roll(x, w - dh // 2, 1), pltpu.roll(x, dh // 2, 1))
        return x * cos + partner * sin

    def chunk(n, states):
        r0 = pl.multiple_of(n * c, c)
        rows = pl.ds(r0, c)
        cos, sin = cos_ref[rows, :], sin_ref[rows, :]
        q = rope(p_ref[rows, 0:w], cos, sin)
        k = rope(p_ref[rows, w:2 * w], cos, sin) * (dh ** -0.5)
        v = p_ref[rows, 2 * w:3 * w]
        g = p_ref[rows, 3 * w:4 * w]
        kz = k * zeta_ref[...]
        gate = g * jax.nn.sigmoid(g) * nw_ref[...]
        new_states = []
        for hd in range(RET_HEADS):
            cols = slice(hd * dh, (hd + 1) * dh)
            qh = q[:, cols].astype(BF16)
            kh = k[:, cols].astype(BF16)
            vh = v[:, cols].astype(BF16)
            scores = _dot(qh, kh, NT_DIMS) * dmat_ref[hd]
            intra = _dot(scores.astype(BF16), vh)
            cross = _dot(qh, states[hd].astype(BF16)) * xi_ref[:, cols]
            kv = _dot(kz[:, cols].astype(BF16), vh, TN_DIMS)
            y = intra + cross
            mu = jnp.mean(y, axis=-1, keepdims=True)
            yc = y - mu
            var = jnp.mean(yc * yc, axis=-1, keepdims=True)
            o_ref[rows, cols] = yc * lax.rsqrt(var + NORM_EPS) * gate[:, cols]
            new_states.append(states[hd] * dec_ref[hd] + kv)
        return tuple(new_states)

    init = tuple(jnp.zeros((dh, dh), F32) for _ in range(RET_HEADS))
    lax.fori_loop(0, seq // c, chunk, init)


def _retention(ret, nw, tables, batch, seq):
    cos_t, sin_t, dmat, zeta_t, xi_t, decay = tables
    w = RET_W
    const2 = lambda b: (0, 0)
    const3 = lambda b: (0, 0, 0)
    return pl.pallas_call(
        _retention_kernel,
        grid=(batch,),
        in_specs=[
            pl.BlockSpec((seq, 4 * w), lambda b: (b, 0)),
            pl.BlockSpec((seq, w), const2),
            pl.BlockSpec((seq, w), const2),
            pl.BlockSpec(dmat.shape, const3),
            pl.BlockSpec(zeta_t.shape, const2),
            pl.BlockSpec(xi_t.shape, const2),
            pl.BlockSpec(decay.shape, const3),
            pl.BlockSpec((1, w), const2),
        ],
        out_specs=pl.BlockSpec((seq, w), lambda b: (b, 0)),
        out_shape=jax.ShapeDtypeStruct((batch * seq, w), F32),
        compiler_params=_params("arbitrary"),
        name="retention",
    )(ret, cos_t, sin_t, dmat, zeta_t, xi_t, decay, nw)


def _moba_kernel(q_ref, k_ref, v_ref, o_ref):
    seq = q_ref.shape[0]
    bk, dh = MOBA_BLOCK, HEAD_DIM
    nb = seq // bk
    scale = dh ** -0.5
    neg = -jnp.inf
    lane = lax.broadcasted_iota(jnp.int32, (1, LANES), 1)
    k_all = k_ref[...]
    k_bf = k_all.astype(BF16)
    v_bf = v_ref[...].astype(BF16)
    kmean = jnp.mean(k_all.reshape(nb, bk, LANES), axis=1)
    colid = lax.broadcasted_iota(jnp.int32, (bk, nb), 1)
    causal = (lax.broadcasted_iota(jnp.int32, (bk, bk), 0)
              >= lax.broadcasted_iota(jnp.int32, (bk, bk), 1))
    heads_per_step = LANES // dh
    outs = [None] * nb
    for p in range(heads_per_step):
        head_lanes = (lane >= p * dh) & (lane < (p + 1) * dh)
        qm = jnp.where(head_lanes, q_ref[...], 0.0)
        gs = _dot3(qm, kmean, NT_DIMS)
        qm_bf = qm.astype(BF16)
        for i in range(nb):
            rows = slice(i * bk, (i + 1) * bk)
            klen = (i + 1) * bk
            s = _dot(qm_bf[rows], k_bf[:klen], NT_DIMS) * scale
            pieces = []
            if i > MOBA_TOPK:
                gsi = gs[rows]
                rank = jnp.zeros((bk, nb), F32)
                for j2 in range(i):
                    cj = gsi[:, j2:j2 + 1]
                    tie = jnp.where(colid > j2, 1.0, 0.0)
                    rank = rank + jnp.where(cj > gsi, 1.0, jnp.where(cj == gsi, tie, 0.0))
                for j in range(i):
                    keep = rank[:, j:j + 1] < float(MOBA_TOPK)
                    pieces.append(jnp.where(keep, s[:, j * bk:(j + 1) * bk], neg))
            else:
                for j in range(i):
                    pieces.append(s[:, j * bk:(j + 1) * bk])
            pieces.append(jnp.where(causal, s[:, i * bk:klen], neg))
            m = pieces[0]
            for pc in pieces[1:]:
                m = jnp.maximum(m, pc)
            m = jnp.max(m, axis=-1, keepdims=True)
            probs = [jnp.exp(pc - m) for pc in pieces]
            tot = probs[0]
            for pr in probs[1:]:
                tot = tot + pr
            denom = jnp.sum(tot, axis=-1, keepdims=True)
            pcat = probs[0] if len(probs) == 1 else jnp.concatenate(probs, axis=1)
            o = _dot(pcat.astype(BF16), v_bf[:klen]) / denom
            outs[i] = o if outs[i] is None else jnp.where(head_lanes, o, outs[i])
    for i in range(nb):
        o_ref[i * bk:(i + 1) * bk, :] = outs[i]


def _moba(moba, batch, seq):
    npair = MOBA_W // LANES
    return pl.pallas_call(
        _moba_kernel,
        grid=(batch, npair),
        in_specs=[
            pl.BlockSpec((seq, LANES), lambda b, p: (b, p)),
            pl.BlockSpec((seq, LANES), lambda b, p: (b, npair + p)),
            pl.BlockSpec((seq, LANES), lambda b, p: (b, 2 * npair + p)),
        ],
        out_specs=pl.BlockSpec((seq, LANES), lambda b, p: (b, p)),
        out_shape=jax.ShapeDtypeStruct((batch * seq, MOBA_W), F32),
        compiler_params=_params("arbitrary", "arbitrary"),
        name="moba",
    )(moba, moba, moba)


def _lru_kernel(xg_ref, cw_ref, cb_ref, wa_ref, ba_ref, wx_ref, bx_ref, lam_ref, o_ref):
    seq = xg_ref.shape[0]
    w = LRU_W
    xb = xg_ref[:, 0:w]
    row = lax.broadcasted_iota(jnp.int32, (seq, w), 0)
    u = cb_ref[...] + cw_ref[CONV_WIDTH - 1:CONV_WIDTH, :] * xb
    for tap in range(CONV_WIDTH - 1):
        back = CONV_WIDTH - 1 - tap
        u = u + cw_ref[tap:tap + 1, :] * jnp.where(row >= back, pltpu.roll(xb, back, 0), 0.0)
    u_bf = u.astype(BF16)
    r = jax.nn.sigmoid(_dot(u_bf, wa_ref[...]) + ba_ref[...])
    ig = jax.nn.sigmoid(_dot(u_bf, wx_ref[...]) + bx_ref[...])
    z = -lam_ref[...]
    softplus = jnp.maximum(z, 0.0) + jnp.log1p(jnp.exp(-jnp.abs(z)))
    log_a = -LRU_C * r * softplus
    a = jnp.exp(log_a)
    b = jnp.sqrt(-jnp.tanh(log_a) * (a * a + 1.0)) * (ig * u)
    d = 1
    while d < seq:
        live = row >= d
        a_prev = jnp.where(live, pltpu.roll(a, d, 0), 1.0)
        b_prev = jnp.where(live, pltpu.roll(b, d, 0), 0.0)
        b = a * b_prev + b
        a = a * a_prev
        d *= 2
    o_ref[...] = jax.nn.gelu(xg_ref[:, w:2 * w]) * b


def _block_diag(wb):
    g, n, _ = wb.shape
    eye = jnp.eye(g, dtype=wb.dtype)
    return (eye[:, None, :, None] * wb[:, :, None, :]).reshape(g * n, g * n)


def _lru(lru, cw, cb, wa, ba, wx, bx, lam, batch, seq):
    w = LRU_W
    const = lambda b: (0, 0)
    vec = pl.BlockSpec((1, w), const)
    return pl.pallas_call(
        _lru_kernel,
        grid=(batch,),
        in_specs=[
            pl.BlockSpec((seq, 2 * w), lambda b: (b, 0)),
            pl.BlockSpec((CONV_WIDTH, w), const),
            vec,
            pl.BlockSpec((w, w), const),
            vec,
            pl.BlockSpec((w, w), const),
            vec,
            vec,
        ],
        out_specs=pl.BlockSpec((seq, w), lambda b: (b, 0)),
        out_shape=jax.ShapeDtypeStruct((batch * seq, w), F32),
        compiler_params=_params("arbitrary"),
        name="rg_lru",
    )(lru, cw, cb.reshape(1, w), _block_diag(wa).astype(BF16), ba.reshape(1, w),
      _block_diag(wx).astype(BF16), bx.reshape(1, w), lam.reshape(1, w))


def _outproj_router_kernel(x_ref, yr_ref, ym_ref, yl_ref, wr_ref, wm_ref, wl_ref, g1_ref, nw_ref,
                           sc_ref, sh_ref, rw_ref, rb_ref,
                           x1_ref, h2_ref, ti_ref, tg_ref, rk_ref, cnt_ref, carry_ref):
    i = pl.program_id(0)
    tm = x_ref.shape[0]
    ne = N_EXPERTS

    @pl.when(i == 0)
    def _():
        carry_ref[...] = jnp.zeros_like(carry_ref)

    mixed = (_dot(yr_ref[...].astype(BF16), wr_ref[...])
             + _dot(ym_ref[...].astype(BF16), wm_ref[...])
             + _dot(yl_ref[...].astype(BF16), wl_ref[...]))
    x1 = x_ref[...] + g1_ref[0] * mixed
    x1_ref[...] = x1
    h2 = _rms_mod(x1, nw_ref[...], sc_ref[0], sh_ref[0])
    h2_ref[...] = h2

    logits = _dot3(rw_ref[...], h2, NT_DIMS) + rb_ref[...]
    eid = lax.broadcasted_iota(jnp.int32, (ne, tm), 0)
    work = logits
    member = jnp.zeros((ne, tm), F32)
    vals, hots = [], []
    for k in range(TOPK_EXPERTS):
        m = jnp.max(work, axis=0, keepdims=True)
        idx = jnp.min(jnp.where(work == m, eid, ne), axis=0, keepdims=True)
        hot = eid == idx
        ti_ref[k:k + 1, :] = idx
        vals.append(m)
        hots.append(hot)
        member = member + jnp.where(hot, 1.0, 0.0)
        work = jnp.where(hot, -jnp.inf, work)
    exps = [jnp.exp(v - vals[0]) for v in vals]
    tot = exps[0]
    for e in exps[1:]:
        tot = tot + e
    for k in range(TOPK_EXPERTS):
        tg_ref[k:k + 1, :] = exps[k] / tot

    before = (lax.broadcasted_iota(jnp.int32, (tm, tm), 0)
              < lax.broadcasted_iota(jnp.int32, (tm, tm), 1))
    prefix = _dot(member.astype(BF16), jnp.where(before, 1.0, 0.0).astype(BF16))
    prefix = prefix + carry_ref[:, 0:1]
    for k in range(TOPK_EXPERTS):
        rk = jnp.sum(jnp.where(hots[k], prefix, 0.0), axis=0, keepdims=True)
        rk_ref[k:k + 1, :] = rk.astype(jnp.int32)
    carry_ref[...] = carry_ref[...] + jnp.sum(member, axis=1, keepdims=True)
    cnt_ref[...] = carry_ref[...].astype(jnp.int32)


def _outproj_router(x2d, y_ret, y_moba, y_lru, wo_bf16, gate1, nw, scale2, shift2, rw_t, rb, seq):
    t, d = x2d.shape
    tm = ROW_TILE
    per_b = seq // tm
    ne, kk = N_EXPERTS, TOPK_EXPERTS
    bmap = lambda i: (i // per_b, 0, 0)
    const = lambda i: (0, 0)
    row = lambda n: pl.BlockSpec((tm, n), lambda i: (i, 0))
    lanes = lambda dt: (pl.BlockSpec((kk, tm), lambda i: (0, i)), jax.ShapeDtypeStruct((kk, t), dt))
    ti_spec, ti_shape = lanes(jnp.int32)
    tg_spec, tg_shape = lanes(F32)
    rk_spec, rk_shape = lanes(jnp.int32)
    w_ret, w_moba, w_lru = wo_bf16[:RET_W], wo_bf16[RET_W:RET_W + MOBA_W], wo_bf16[RET_W + MOBA_W:]
    return pl.pallas_call(
        _outproj_router_kernel,
        grid=(t // tm,),
        in_specs=[
            row(d), row(RET_W), row(MOBA_W), row(LRU_W),
            pl.BlockSpec(w_ret.shape, const), pl.BlockSpec(w_moba.shape, const),
            pl.BlockSpec(w_lru.shape, const),
            pl.BlockSpec((1, 1, d), bmap),
            pl.BlockSpec((1, d), const),
            pl.BlockSpec((1, 1, d), bmap),
            pl.BlockSpec((1, 1, d), bmap),
            pl.BlockSpec((ne, d), const),
            pl.BlockSpec((ne, 1), const),
        ],
        out_specs=[
            row(d), row(d), ti_spec, tg_spec, rk_spec,
            pl.BlockSpec((ne, LANES), const),
        ],
        out_shape=[
            jax.ShapeDtypeStruct((t, d), F32),
            jax.ShapeDtypeStruct((t, d), F32),
            ti_shape, tg_shape, rk_shape,
            jax.ShapeDtypeStruct((ne, LANES), jnp.int32),
        ],
        scratch_shapes=[pltpu.VMEM((ne, LANES), F32)],
        compiler_params=_params("arbitrary"),
        name="outproj_router",
    )(x2d, y_ret, y_moba, y_lru, w_ret, w_moba, w_lru, gate1, nw, scale2, shift2, rw_t, rb)


def _row_copy(src_ref, src_row, dst_ref, dst_row, sem):
    return pltpu.make_async_copy(src_ref.at[pl.ds(src_row, 1)], dst_ref.at[pl.ds(dst_row, 1)], sem)


def _dispatch_kernel(tok_ref, h_ref, xr_ref, sem):
    tm = xr_ref.shape[0]

    def issue(r, carry):
        _row_copy(h_ref, tok_ref[0, r], xr_ref, r, sem).start()
        return carry

    def drain(r, carry):
        _row_copy(h_ref, tok_ref[0, r], xr_ref, r, sem).wait()
        return carry

    lax.fori_loop(0, tm, issue, 0)
    lax.fori_loop(0, tm, drain, 0)


def _dispatch(row_tok, h2, n_rows):
    t, d = h2.shape
    tm = ROW_TILE
    return pl.pallas_call(
        _dispatch_kernel,
        grid=(n_rows // tm,),
        in_specs=[
            pl.BlockSpec((1, tm), lambda i: (0, i), memory_space=pltpu.SMEM),
            pl.BlockSpec(memory_space=pl.ANY),
        ],
        out_specs=pl.BlockSpec((tm, d), lambda i: (i, 0)),
        out_shape=jax.ShapeDtypeStruct((n_rows, d), F32),
        scratch_shapes=[pltpu.SemaphoreType.DMA(())],
        compiler_params=_params("arbitrary"),
        name="dispatch",
    )(row_tok.reshape(1, n_rows), h2)


def _expert_kernel(be_ref, nu_ref, x_ref, wgu_ref, bgu_ref, wdn_ref, bdn_ref, y_ref,
                   wgu_bf, wdn_bf):
    i = pl.program_id(0)
    ff = wdn_ref.shape[1]

    @pl.when(i >= nu_ref[0])
    def _():
        y_ref[...] = jnp.zeros_like(y_ref)

    @pl.when(i < nu_ref[0])
    def _():
        prev = be_ref[jnp.maximum(i - 1, 0)]

        @pl.when((i == 0) | (be_ref[i] != prev))
        def _():
            wgu_bf[...] = wgu_ref[0].astype(BF16)
            wdn_bf[...] = wdn_ref[0].astype(BF16)

        gu = _dot(x_ref[...].astype(BF16), wgu_bf[...]) + bgu_ref[0]
        gate = jnp.minimum(gu[:, :ff], SWIGLU_LIMIT)
        up = jnp.clip(gu[:, ff:], -SWIGLU_LIMIT, SWIGLU_LIMIT)
        act = (up + 1.0) * gate * jax.nn.sigmoid(SWIGLU_ALPHA * gate)
        y_ref[...] = _dot(act.astype(BF16), wdn_bf[...]) + bdn_ref[0]


def _experts(blk_e, n_used, xr, w_gu, b_gu, w_dn, b_dn):
    n_rows, d = xr.shape
    ne, _, ff2 = w_gu.shape
    ff = ff2 // 2
    tm = EXPERT_ROWS
    nblk = n_rows // tm

    def last_used(i, nu):
        return jnp.minimum(i, nu[0] - 1)

    xmap = lambda i, be, nu: (last_used(i, nu), 0)
    wmap = lambda i, be, nu: (be[last_used(i, nu)], 0, 0)
    grid_spec = pltpu.PrefetchScalarGridSpec(
        num_scalar_prefetch=2,
        grid=(nblk,),
        in_specs=[
            pl.BlockSpec((tm, d), xmap),
            pl.BlockSpec((1, d, ff2), wmap),
            pl.BlockSpec((1, 1, ff2), wmap),
            pl.BlockSpec((1, ff, d), wmap),
            pl.BlockSpec((1, 1, d), wmap),
        ],
        out_specs=pl.BlockSpec((tm, d), lambda i, be, nu: (i, 0)),
        scratch_shapes=[pltpu.VMEM((d, ff2), BF16), pltpu.VMEM((ff, d), BF16)],
    )
    return pl.pallas_call(
        _expert_kernel,
        grid_spec=grid_spec,
        out_shape=jax.ShapeDtypeStruct((n_rows, d), F32),
        compiler_params=_params("arbitrary"),
        name="experts",
    )(blk_e, n_used, xr, w_gu, b_gu.reshape(ne, 1, ff2), w_dn, b_dn.reshape(ne, 1, d))


def _combine_kernel(pos_ref, x1_ref, g_ref, g2_ref, fw_ref, yr_ref, o_ref, buf, sem, *, final_norm):
    tm = x1_ref.shape[0]

    def issue(t, carry):
        for k in range(TOPK_EXPERTS):
            _row_copy(yr_ref, pos_ref[k, t], buf.at[k], t, sem).start()
        return carry

    def drain(t, carry):
        for k in range(TOPK_EXPERTS):
            _row_copy(yr_ref, pos_ref[k, t], buf.at[k], t, sem).wait()
        return carry

    lax.fori_loop(0, tm, issue, 0)
    lax.fori_loop(0, tm, drain, 0)
    acc = g_ref[:, 0:1] * buf[0]
    for k in range(1, TOPK_EXPERTS):
        acc = acc + g_ref[:, k:k + 1] * buf[k]
    x2 = x1_ref[...] + g2_ref[0] * acc
    if final_norm:
        ms = jnp.mean(x2 * x2, axis=-1, keepdims=True)
        x2 = x2 * lax.rsqrt(ms + NORM_EPS) * fw_ref[...]
    o_ref[...] = x2


def _combine(pos, x1, gates_t, gate2, final_w, yr, seq, final_norm):
    t, d = x1.shape
    tm = COMBINE_ROWS
    per_b = seq // tm
    kk = TOPK_EXPERTS
    return pl.pallas_call(
        functools.partial(_combine_kernel, final_norm=final_norm),
        grid=(t // tm,),
        in_specs=[
            pl.BlockSpec((kk, tm), lambda i: (0, i), memory_space=pltpu.SMEM),
            pl.BlockSpec((tm, d), lambda i: (i, 0)),
            pl.BlockSpec((tm, kk), lambda i: (i, 0)),
            pl.BlockSpec((1, 1, d), lambda i: (i // per_b, 0, 0)),
            pl.BlockSpec((1, d), lambda i: (0, 0)),
            pl.BlockSpec(memory_space=pl.ANY),
        ],
        out_specs=pl.BlockSpec((tm, d), lambda i: (i, 0)),
        out_shape=jax.ShapeDtypeStruct((t, d), F32),
        scratch_shapes=[pltpu.VMEM((kk, tm, d), F32), pltpu.SemaphoreType.DMA(())],
        compiler_params=_params("arbitrary"),
        name="combine",
    )(pos, x1, gates_t, gate2, final_w, yr)


def _routing_tables(top_i, rank, counts, n_blocks):
    blk = EXPERT_ROWS
    padded = ((counts + blk - 1) // blk) * blk
    pend = jnp.cumsum(padded)
    pstart = pend - padded
    pos = pstart[top_i] + rank
    starts = jnp.arange(n_blocks, dtype=jnp.int32) * blk
    blk_e = jnp.minimum(jnp.searchsorted(pend, starts, side="right"), N_EXPERTS - 1).astype(jnp.int32)
    n_used = (pend[-1:] // blk).astype(jnp.int32)
    pos = pos.astype(jnp.int32)
    tok = jnp.broadcast_to(jnp.arange(pos.shape[1], dtype=jnp.int32), pos.shape)
    row_tok = jnp.zeros((n_blocks * blk,), jnp.int32).at[pos.reshape(-1)].set(tok.reshape(-1))
    return pos, row_tok, blk_e, n_used


def kernel(x, c, ada_w, ada_b, norm_mix_w, w_in, ret_norm_w, lru_conv_w, lru_conv_b, lru_gate_a_w,
           lru_gate_a_b, lru_gate_x_w, lru_gate_x_b, lru_lambda, w_out, norm_ffn_w, router_w, router_b,
           moe_w_gu, moe_b_gu, moe_w_down, moe_b_down, final_norm_w):
    batch, seq, d = x.shape
    depth = ada_w.shape[0]
    t = batch * seq
    n_blocks = (t * TOPK_EXPERTS) // EXPERT_ROWS + N_EXPERTS
    tables = _retention_tables(seq)
    mod = _ada_mod(c, ada_w, ada_b)
    xf = x.reshape(t, d)
    for l in range(depth):
        shift1, scale1, gate1, shift2, scale2, gate2 = [
            mod[l, :, j * d:(j + 1) * d].reshape(batch, 1, d) for j in range(6)]
        ret, moba, lru = _inproj(xf, norm_mix_w[l].reshape(1, d), scale1, shift1,
                                 w_in[l].astype(BF16), seq)
        y_ret = _retention(ret, ret_norm_w[l].reshape(1, RET_W), tables, batch, seq)
        y_moba = _moba(moba, batch, seq)
        y_lru = _lru(lru, lru_conv_w[l], lru_conv_b[l], lru_gate_a_w[l], lru_gate_a_b[l],
                     lru_gate_x_w[l], lru_gate_x_b[l], lru_lambda[l], batch, seq)
        x1, h2, top_i, top_g, rank, counts = _outproj_router(
            xf, y_ret, y_moba, y_lru, w_out[l].astype(BF16), gate1, norm_ffn_w[l].reshape(1, d),
            scale2, shift2, router_w[l].T, router_b[l].reshape(N_EXPERTS, 1), seq)
        pos, row_tok, blk_e, n_used = _routing_tables(top_i, rank, counts[:, 0], n_blocks)
        xr = _dispatch(row_tok, h2, n_blocks * EXPERT_ROWS)
        yr = _experts(blk_e, n_used, xr, moe_w_gu[l], moe_b_gu[l], moe_w_down[l], moe_b_down[l])
        xf = _combine(pos, x1, top_g.T, gate2, final_norm_w.reshape(1, d), yr, seq,
                      final_norm=(l == depth - 1))
    return xf.reshape(batch, seq, d)
```

```python
import functools

import jax
import jax.numpy as jnp
from jax import lax
from jax.experimental import pallas as pl
from jax.experimental.pallas import tpu as pltpu

F32 = jnp.float32
BF16 = jnp.bfloat16
I32 = jnp.int32

HEAD_DIM = 64
RET_HEADS = 4
RET_W = RET_HEADS * HEAD_DIM
RET_CHUNK = 128
ROPE_BASE = 10000.0
MOBA_HEADS = 6
MOBA_W = MOBA_HEADS * HEAD_DIM
MOBA_BLOCK = 256
MOBA_TOPK = 3
LRU_BLOCKS = 6
LRU_BLOCK_W = 64
LRU_W = LRU_BLOCKS * LRU_BLOCK_W
CONV_WIDTH = 4
LRU_C = 8.0
N_EXPERTS = 32
TOPK_EXPERTS = 4
SWIGLU_LIMIT = 7.0
SWIGLU_ALPHA = 1.702
NORM_EPS = 1e-6
N_MOD = 6

LANES = 128
VMEM_LIMIT = 56 * 1024 * 1024
ROW_TILE = 512
EXPERT_ROWS = 256
MOVE_ROWS = 256
MOVE_UNROLL = 8
PLAN_ROWS = 2048

NT_DIMS = (((1,), (1,)), ((), ()))
NN_DIMS = (((1,), (0,)), ((), ()))
TN_DIMS = (((0,), (0,)), ((), ()))


def _params(*semantics):
    return pltpu.CompilerParams(dimension_semantics=semantics, vmem_limit_bytes=VMEM_LIMIT)


def _dot(a, b, dims=NN_DIMS):
    return lax.dot_general(a, b, dims, preferred_element_type=F32)


def _split_bf16(a):
    hi = a.astype(BF16)
    lo = (a - hi.astype(F32)).astype(BF16)
    return hi, lo


def _dot3(a, b, dims=NN_DIMS):
    ah, al = _split_bf16(a)
    bh, bl = _split_bf16(b)
    return _dot(ah, bl, dims) + _dot(al, bh, dims) + _dot(ah, bh, dims)


def _rms_mod(x, nw, scale, shift):
    ms = jnp.mean(x * x, axis=-1, keepdims=True)
    y = x * lax.rsqrt(ms + NORM_EPS) * nw
    return y * (1.0 + scale) + shift


def _mod_spec(layer, chunk, d, per_b):
    return pl.BlockSpec((1, 1, 1, d), lambda i: (layer, i // per_b, 0, chunk))


def _layer_spec(layer, shape):
    zeros = (0,) * len(shape)
    return pl.BlockSpec((1,) + tuple(shape), lambda *_: (layer,) + zeros)


def _ada_kernel(c_ref, w_ref, b_ref, o_ref):
    c = c_ref[...]
    c_act = c * jax.nn.sigmoid(c)
    o_ref[0] = _dot3(c_act, w_ref[0]) + b_ref[0]


def _ada_mod(c, ada_w, ada_b):
    depth, d, n = ada_w.shape
    b = c.shape[0]
    tn = n // 4
    return pl.pallas_call(
        _ada_kernel,
        grid=(depth, n // tn),
        in_specs=[
            pl.BlockSpec((b, d), lambda l, j: (0, 0)),
            pl.BlockSpec((1, d, tn), lambda l, j: (l, 0, j)),
            pl.BlockSpec((1, 1, tn), lambda l, j: (l, 0, j)),
        ],
        out_specs=pl.BlockSpec((1, b, tn), lambda l, j: (l, 0, j)),
        out_shape=jax.ShapeDtypeStruct((depth, b, n), F32),
        compiler_params=_params("arbitrary", "arbitrary"),
        name="ada_mod",
    )(c, ada_w, ada_b.reshape(depth, 1, n))


def _inproj_kernel(x_ref, nw_ref, sc_ref, sh_ref, w_ref, ret_ref, moba_ref, lru_ref):
    h = _rms_mod(x_ref[...], nw_ref[0], sc_ref[0, 0], sh_ref[0, 0]).astype(BF16)
    o0 = RET_W * 4
    o1 = o0 + MOBA_W * 3
    ret_ref[...] = _dot(h, w_ref[0, :, 0:o0])
    moba_ref[...] = _dot(h, w_ref[0, :, o0:o1])
    lru_ref[...] = _dot(h, w_ref[0, :, o1:])


def _inproj(layer, x2d, norm_w, mod, w_bf16, seq):
    t, d = x2d.shape
    tm = ROW_TILE
    per_b = seq // tm
    n_ret, n_moba, n_lru = RET_W * 4, MOBA_W * 3, LRU_W * 2
    return pl.pallas_call(
        _inproj_kernel,
        grid=(t // tm,),
        in_specs=[
            pl.BlockSpec((tm, d), lambda i: (i, 0)),
            _layer_spec(layer, (1, d)),
            _mod_spec(layer, 1, d, per_b),
            _mod_spec(layer, 0, d, per_b),
            _layer_spec(layer, w_bf16.shape[1:]),
        ],
        out_specs=[
            pl.BlockSpec((tm, n_ret), lambda i: (i, 0)),
            pl.BlockSpec((tm, n_moba), lambda i: (i, 0)),
            pl.BlockSpec((tm, n_lru), lambda i: (i, 0)),
        ],
        out_shape=[
            jax.ShapeDtypeStruct((t, n_ret), F32),
            jax.ShapeDtypeStruct((t, n_moba), F32),
            jax.ShapeDtypeStruct((t, n_lru), F32),
        ],
        compiler_params=_params("arbitrary"),
        name="inproj",
    )(x2d, norm_w, mod, mod, w_bf16)


def _retention_tables(seq):
    h, dh, c = RET_HEADS, HEAD_DIM, RET_CHUNK
    half = dh // 2
    inv = 1.0 / (ROPE_BASE ** (jnp.arange(half, dtype=F32) / half))
    ang = jnp.arange(seq).astype(F32)[:, None] * inv[None, :]
    cos, sin = jnp.cos(ang), jnp.sin(ang)
    cos_t = jnp.tile(jnp.concatenate([cos, cos], axis=-1), (1, h))
    sin_t = jnp.tile(jnp.concatenate([-sin, sin], axis=-1), (1, h))
    log_g = jnp.log1p(-jnp.exp2(-5.0 - jnp.arange(h, dtype=F32)))
    idx = jnp.arange(c, dtype=F32)
    diff = idx[:, None] - idx[None, :]
    dmat = jnp.where(diff >= 0, jnp.exp(log_g[:, None, None] * jnp.maximum(diff, 0.0)), 0.0)
    zeta = jnp.exp(log_g[:, None] * (c - 1 - idx)[None, :])
    xi = jnp.exp(log_g[:, None] * (idx + 1)[None, :])
    zeta_t = jnp.repeat(zeta.T, dh, axis=1)
    xi_t = jnp.repeat(xi.T, dh, axis=1)
    decay = jnp.broadcast_to(jnp.exp(log_g * c)[:, None, None], (h, dh, dh))
    return cos_t, sin_t, dmat, zeta_t, xi_t, decay


def _retention_kernel(p_ref, cos_ref, sin_ref, dmat_ref, zeta_ref, xi_ref, dec_ref, nw_ref, o_ref):
    seq = p_ref.shape[0]
    c, dh, w = RET_CHUNK, HEAD_DIM, RET_W
    lane = lax.broadcasted_iota(I32, (1, w), 1)
    first_half = (lane % dh) < (dh // 2)

    def rope(x, cos, sin):
        partner = jnp.where(first_half, pltpu.roll(x, w - dh // 2, 1), pltpu.roll(x, dh // 2, 1))
        return x * cos + partner * sin

    def chunk(n, states):
        r0 = pl.multiple_of(n * c, c)
        rows = pl.ds(r0, c)
        cos, sin = cos_ref[rows, :], sin_ref[rows, :]
        q = rope(p_ref[rows, 0:w], cos, sin)
        k = rope(p_ref[rows, w:2 * w], cos, sin) * (dh ** -0.5)
        v = p_ref[rows, 2 * w:3 * w]
        g = p_ref[rows, 3 * w:4 * w]
        kz = k * zeta_ref[...]
        gate = g * jax.nn.sigmoid(g) * nw_ref[0]
        new_states = []
        for hd in range(RET_HEADS):
            cols = slice(hd * dh, (hd + 1) * dh)
            qh = q[:, cols].astype(BF16)
            kh = k[:, cols].astype(BF16)
            vh = v[:, cols].astype(BF16)
            scores = _dot(qh, kh, NT_DIMS) * dmat_ref[hd]
            intra = _dot(scores.astype(BF16), vh)
            cross = _dot(qh, states[hd].astype(BF16)) * xi_ref[:, cols]
            kv = _dot(kz[:, cols].astype(BF16), vh, TN_DIMS)
            y = intra + cross
            mu = jnp.mean(y, axis=-1, keepdims=True)
            yc = y - mu
            var = jnp.mean(yc * yc, axis=-1, keepdims=True)
            o_ref[rows, cols] = yc * lax.rsqrt(var + NORM_EPS) * gate[:, cols]
            new_states.append(states[hd] * dec_ref[hd] + kv)
        return tuple(new_states)

    init = tuple(jnp.zeros((dh, dh), F32) for _ in range(RET_HEADS))
    lax.fori_loop(0, seq // c, chunk, init)


def _retention(layer, ret, norm_w, tables, batch, seq):
    cos_t, sin_t, dmat, zeta_t, xi_t, decay = tables
    w = RET_W
    const2 = lambda b: (0, 0)
    const3 = lambda b: (0, 0, 0)
    return pl.pallas_call(
        _retention_kernel,
        grid=(batch,),
        in_specs=[
            pl.BlockSpec((seq, 4 * w), lambda b: (b, 0)),
            pl.BlockSpec((seq, w), const2),
            pl.BlockSpec((seq, w), const2),
            pl.BlockSpec(dmat.shape, const3),
            pl.BlockSpec(zeta_t.shape, const2),
            pl.BlockSpec(xi_t.shape, const2),
            pl.BlockSpec(decay.shape, const3),
            _layer_spec(layer, (1, w)),
        ],
        out_specs=pl.BlockSpec((seq, w), lambda b: (b, 0)),
        out_shape=jax.ShapeDtypeStruct((batch * seq, w), F32),
        compiler_params=_params("arbitrary"),
        name="retention",
    )(ret, cos_t, sin_t, dmat, zeta_t, xi_t, decay, norm_w)


def _moba_kernel(q_ref, k_ref, v_ref, o_ref):
    seq = q_ref.shape[0]
    bk, dh = MOBA_BLOCK, HEAD_DIM
    nb = seq // bk
    scale = dh ** -0.5
    neg = -jnp.inf
    lane = lax.broadcasted_iota(I32, (1, LANES), 1)
    k_all = k_ref[...]
    k_bf = k_all.astype(BF16)
    v_bf = v_ref[...].astype(BF16)
    kmean = jnp.mean(k_all.reshape(nb, bk, LANES), axis=1)
    colid = lax.broadcasted_iota(I32, (bk, nb), 1)
    causal = (lax.broadcasted_iota(I32, (bk, bk), 0)
              >= lax.broadcasted_iota(I32, (bk, bk), 1))
    heads_per_step = LANES // dh
    outs = [None] * nb
    for p in range(heads_per_step):
        head_lanes = (lane >= p * dh) & (lane < (p + 1) * dh)
        qm = jnp.where(head_lanes, q_ref[...], 0.0)
        gs = _dot3(qm, kmean, NT_DIMS)
        qm_bf = qm.astype(BF16)
        for i in range(nb):
            rows = slice(i * bk, (i + 1) * bk)
            klen = (i + 1) * bk
            s = _dot(qm_bf[rows], k_bf[:klen], NT_DIMS) * scale
            pieces = []
            if i > MOBA_TOPK:
                gsi = gs[rows]
                rank = jnp.zeros((bk, nb), F32)
                for j2 in range(i):
                    cj = gsi[:, j2:j2 + 1]
                    tie = jnp.where(colid > j2, 1.0, 0.0)
                    rank = rank + jnp.where(cj > gsi, 1.0, jnp.where(cj == gsi, tie, 0.0))
                for j in range(i):
                    keep = rank[:, j:j + 1] < float(MOBA_TOPK)
                    pieces.append(jnp.where(keep, s[:, j * bk:(j + 1) * bk], neg))
            else:
                for j in range(i):
                    pieces.append(s[:, j * bk:(j + 1) * bk])
            pieces.append(jnp.where(causal, s[:, i * bk:klen], neg))
            m = pieces[0]
            for pc in pieces[1:]:
                m = jnp.maximum(m, pc)
            m = jnp.max(m, axis=-1, keepdims=True)
            probs = [jnp.exp(pc - m) for pc in pieces]
            tot = probs[0]
            for pr in probs[1:]:
                tot = tot + pr
            denom = jnp.sum(tot, axis=-1, keepdims=True)
            pcat = probs[0] if len(probs) == 1 else jnp.concatenate(probs, axis=1)
            o = _dot(pcat.astype(BF16), v_bf[:klen]) / denom
            outs[i] = o if outs[i] is None else jnp.where(head_lanes, o, outs[i])
    for i in range(nb):
        o_ref[i * bk:(i + 1) * bk, :] = outs[i]


def _moba(moba, batch, seq):
    npair = MOBA_W // LANES
    return pl.pallas_call(
        _moba_kernel,
        grid=(batch, npair),
        in_specs=[
            pl.BlockSpec((seq, LANES), lambda b, p: (b, p)),
            pl.BlockSpec((seq, LANES), lambda b, p: (b, npair + p)),
            pl.BlockSpec((seq, LANES), lambda b, p: (b, 2 * npair + p)),
        ],
        out_specs=pl.BlockSpec((seq, LANES), lambda b, p: (b, p)),
        out_shape=jax.ShapeDtypeStruct((batch * seq, MOBA_W), F32),
        compiler_params=_params("arbitrary", "arbitrary"),
        name="moba",
    )(moba, moba, moba)


def _lru_kernel(xg_ref, cw_ref, cb_ref, wa_ref, ba_ref, wx_ref, bx_ref, lam_ref, o_ref):
    seq = xg_ref.shape[0]
    w = LRU_W
    xb = xg_ref[:, 0:w]
    row = lax.broadcasted_iota(I32, (seq, w), 0)
    u = cb_ref[0] + cw_ref[0, CONV_WIDTH - 1:CONV_WIDTH, :] * xb
    for tap in range(CONV_WIDTH - 1):
        back = CONV_WIDTH - 1 - tap
        u = u + cw_ref[0, tap:tap + 1, :] * jnp.where(row >= back, pltpu.roll(xb, back, 0), 0.0)
    u_bf = u.astype(BF16)
    r = jax.nn.sigmoid(_dot(u_bf, wa_ref[0]) + ba_ref[0])
    ig = jax.nn.sigmoid(_dot(u_bf, wx_ref[0]) + bx_ref[0])
    z = -lam_ref[0]
    softplus = jnp.maximum(z, 0.0) + jnp.log1p(jnp.exp(-jnp.abs(z)))
    log_a = -LRU_C * r * softplus
    a = jnp.exp(log_a)
    b = jnp.sqrt(-jnp.tanh(log_a) * (a * a + 1.0)) * (ig * u)
    d = 1
    while d < seq:
        live = row >= d
        a_prev = jnp.where(live, pltpu.roll(a, d, 0), 1.0)
        b_prev = jnp.where(live, pltpu.roll(b, d, 0), 0.0)
        b = a * b_prev + b
        a = a * a_prev
        d *= 2
    o_ref[...] = jax.nn.gelu(xg_ref[:, w:2 * w]) * b


def _block_diag(wb):
    depth, g, n, _ = wb.shape
    eye = jnp.eye(g, dtype=wb.dtype)
    return (eye[None, :, None, :, None] * wb[:, :, :, None, :]).reshape(depth, g * n, g * n)


def _lru(layer, lru, cw, cb, wa_bd, ba, wx_bd, bx, lam, batch, seq):
    w = LRU_W
    vec = _layer_spec(layer, (1, w))
    mat = _layer_spec(layer, (w, w))
    return pl.pallas_call(
        _lru_kernel,
        grid=(batch,),
        in_specs=[
            pl.BlockSpec((seq, 2 * w), lambda b: (b, 0)),
            _layer_spec(layer, (CONV_WIDTH, w)),
            vec, mat, vec, mat, vec, vec,
        ],
        out_specs=pl.BlockSpec((seq, w), lambda b: (b, 0)),
        out_shape=jax.ShapeDtypeStruct((batch * seq, w), F32),
        compiler_params=_params("arbitrary"),
        name="rg_lru",
    )(lru, cw, cb, wa_bd, ba, wx_bd, bx, lam)


def _outproj_router_kernel(x_ref, yr_ref, ym_ref, yl_ref, wo_ref, g1_ref, nw_ref,
                           sc_ref, sh_ref, rw_ref, rb_ref,
                           x1_ref, h2_ref, ti_ref, tg_ref, rk_ref, cnt_ref, carry_ref):
    i = pl.program_id(0)
    tm = x_ref.shape[0]
    ne = N_EXPERTS
    o0, o1 = RET_W, RET_W + MOBA_W

    @pl.when(i == 0)
    def _():
        carry_ref[...] = jnp.zeros_like(carry_ref)

    mixed = (_dot(yr_ref[...].astype(BF16), wo_ref[0, 0:o0, :])
             + _dot(ym_ref[...].astype(BF16), wo_ref[0, o0:o1, :])
             + _dot(yl_ref[...].astype(BF16), wo_ref[0, o1:, :]))
    x1 = x_ref[...] + g1_ref[0, 0] * mixed
    x1_ref[...] = x1
    h2 = _rms_mod(x1, nw_ref[0], sc_ref[0, 0], sh_ref[0, 0])
    h2_ref[...] = h2

    logits = _dot3(rw_ref[0], h2, NT_DIMS) + rb_ref[0]
    eid = lax.broadcasted_iota(I32, (ne, tm), 0)
    work = logits
    member = jnp.zeros((ne, tm), F32)
    vals, hots = [], []
    for k in range(TOPK_EXPERTS):
        m = jnp.max(work, axis=0, keepdims=True)
        idx = jnp.min(jnp.where(work == m, eid, ne), axis=0, keepdims=True)
        hot = eid == idx
        ti_ref[k:k + 1, :] = idx
        vals.append(m)
        hots.append(hot)
        member = member + jnp.where(hot, 1.0, 0.0)
        work = jnp.where(hot, -jnp.inf, work)
    exps = [jnp.exp(v - vals[0]) for v in vals]
    tot = exps[0]
    for e in exps[1:]:
        tot = tot + e
    for k in range(TOPK_EXPERTS):
        tg_ref[k:k + 1, :] = exps[k] / tot

    before = (lax.broadcasted_iota(I32, (tm, tm), 0)
              < lax.broadcasted_iota(I32, (tm, tm), 1))
    prefix = _dot(member.astype(BF16), jnp.where(before, 1.0, 0.0).astype(BF16))
    prefix = prefix + carry_ref[:, 0:1]
    for k in range(TOPK_EXPERTS):
        rk = jnp.sum(jnp.where(hots[k], prefix, 0.0), axis=0, keepdims=True)
        rk_ref[k:k + 1, :] = rk.astype(I32)
    carry_ref[...] = carry_ref[...] + jnp.sum(member, axis=1, keepdims=True)
    cnt_ref[...] = carry_ref[...].astype(I32)


def _outproj_router(layer, x2d, y_ret, y_moba, y_lru, wo_bf16, mod, norm_w, rw_t, rb, seq):
    t, d = x2d.shape
    tm = ROW_TILE
    per_b = seq // tm
    ne, kk = N_EXPERTS, TOPK_EXPERTS
    const = lambda i: (0, 0)
    row = lambda n: pl.BlockSpec((tm, n), lambda i: (i, 0))
    lanes = pl.BlockSpec((kk, tm), lambda i: (0, i))
    return pl.pallas_call(
        _outproj_router_kernel,
        grid=(t // tm,),
        in_specs=[
            row(d), row(RET_W), row(MOBA_W), row(LRU_W),
            _layer_spec(layer, wo_bf16.shape[1:]),
            _mod_spec(layer, 2, d, per_b),
            _layer_spec(layer, (1, d)),
            _mod_spec(layer, 4, d, per_b),
            _mod_spec(layer, 3, d, per_b),
            _layer_spec(layer, (ne, d)),
            _layer_spec(layer, (ne, 1)),
        ],
        out_specs=[
            row(d), row(d), lanes, lanes, lanes,
            pl.BlockSpec((ne, LANES), const),
        ],
        out_shape=[
            jax.ShapeDtypeStruct((t, d), F32),
            jax.ShapeDtypeStruct((t, d), F32),
            jax.ShapeDtypeStruct((kk, t), I32),
            jax.ShapeDtypeStruct((kk, t), F32),
            jax.ShapeDtypeStruct((kk, t), I32),
            jax.ShapeDtypeStruct((ne, LANES), I32),
        ],
        scratch_shapes=[pltpu.VMEM((ne, LANES), F32)],
        compiler_params=_params("arbitrary"),
        name="outproj_router",
    )(x2d, y_ret, y_moba, y_lru, wo_bf16, mod, norm_w, mod, mod, rw_t, rb)


def _plan_kernel(ti_ref, rk_ref, cnt_ref, pos_ref, blk_ref, meta_ref):
    ne = N_EXPERTS
    tm = ti_ref.shape[1]
    nblk_pad = blk_ref.shape[1]
    shift = EXPERT_ROWS.bit_length() - 1
    counts = cnt_ref[...]
    padded = ((counts + (EXPERT_ROWS - 1)) >> shift) << shift
    row = lax.broadcasted_iota(I32, (ne, LANES), 0)
    pend = padded
    d = 1
    while d < ne:
        pend = pend + jnp.where(row >= d, pltpu.roll(pend, d, 0), 0)
        d *= 2
    pstart = (pend - padded)[:, 0:1]
    eid = lax.broadcasted_iota(I32, (ne, tm), 0)
    for k in range(TOPK_EXPERTS):
        base = jnp.sum(jnp.where(ti_ref[k:k + 1, :] == eid, pstart, 0), axis=0, keepdims=True)
        pos_ref[k:k + 1, :] = base + rk_ref[k:k + 1, :]
    starts = lax.broadcasted_iota(I32, (ne, nblk_pad), 1) * EXPERT_ROWS
    owner = jnp.sum(jnp.where(pend[:, 0:1] <= starts, 1, 0), axis=0, keepdims=True)
    blk_ref[...] = jnp.minimum(owner, ne - 1)
    lane = lax.broadcasted_iota(I32, (ne, LANES), 1)
    pend_lanes = jnp.sum(jnp.where(row == lane, pend, 0), axis=0, keepdims=True)
    total = jnp.max(pend, axis=0, keepdims=True)
    meta_ref[0:1, :] = pend_lanes
    meta_ref[1:2, :] = total >> shift
    meta_ref[2:3, :] = jnp.sum(jnp.where(row == lane, padded, 0), axis=0, keepdims=True)
    meta_ref[3:8, :] = jnp.zeros((5, LANES), I32)


def _plan(top_i, rank, counts, n_blocks):
    kk, t = top_i.shape
    tm = PLAN_ROWS
    nblk_pad = -(-n_blocks // LANES) * LANES
    const = lambda i: (0, 0)
    return pl.pallas_call(
        _plan_kernel,
        grid=(t // tm,),
        in_specs=[
            pl.BlockSpec((kk, tm), lambda i: (0, i)),
            pl.BlockSpec((kk, tm), lambda i: (0, i)),
            pl.BlockSpec(counts.shape, const),
        ],
        out_specs=[
            pl.BlockSpec((kk, tm), lambda i: (0, i)),
            pl.BlockSpec((1, nblk_pad), const),
            pl.BlockSpec((8, LANES), const),
        ],
        out_shape=[
            jax.ShapeDtypeStruct((kk, t), I32),
            jax.ShapeDtypeStruct((1, nblk_pad), I32),
            jax.ShapeDtypeStruct((8, LANES), I32),
        ],
        compiler_params=_params("arbitrary"),
        name="plan",
    )(top_i, rank, counts)


def _rows_copy(src_ref, src_row, dst_ref, dst_row, sem, n=1):
    return pltpu.make_async_copy(src_ref.at[pl.ds(src_row, n)], dst_ref.at[pl.ds(dst_row, n)], sem)


def _dispatch_kernel(meta_ref, pos_ref, h_ref, xr_ref, zeros_ref, fill_sem, row_sem):
    i = pl.program_id(0)
    last = pl.num_programs(0) - 1
    tm = pos_ref.shape[1]
    blk = EXPERT_ROWS
    n_blocks = xr_ref.shape[0] // blk

    def fill(dst_row):
        return _rows_copy(zeros_ref, 0, xr_ref, pl.multiple_of(dst_row, blk), fill_sem, blk)

    def owns_rows(e):
        return meta_ref[2, e] > 0

    @pl.when(i == 0)
    def _():
        zeros_ref[...] = jnp.zeros_like(zeros_ref)
        for e in range(N_EXPERTS):
            @pl.when(owns_rows(e))
            def _():
                fill(meta_ref[0, e] - blk).start()

        def tail_start(b, carry):
            fill(b * blk).start()
            return carry

        def tail_wait(b, carry):
            fill(b * blk).wait()
            return carry

        lax.fori_loop(meta_ref[1, 0], n_blocks, tail_start, 0)
        for e in range(N_EXPERTS):
            @pl.when(owns_rows(e))
            def _():
                fill(meta_ref[0, e] - blk).wait()
        lax.fori_loop(meta_ref[1, 0], n_blocks, tail_wait, 0)

    base = i * tm

    def issue(g, carry):
        for u in range(MOVE_UNROLL):
            t = g * MOVE_UNROLL + u
            for k in range(TOPK_EXPERTS):
                _rows_copy(h_ref, base + t, xr_ref, pos_ref[k, t], row_sem).start(priority=(u + k) % 2)
        return carry

    def drain(g, carry):
        for u in range(MOVE_UNROLL * TOPK_EXPERTS):
            _rows_copy(h_ref, 0, xr_ref, 0, row_sem).wait()
        return carry

    lax.fori_loop(0, tm // MOVE_UNROLL, issue, 0)

    @pl.when(i > 0)
    def _():
        lax.fori_loop(0, tm // MOVE_UNROLL, drain, 0)

    @pl.when(i == last)
    def _():
        lax.fori_loop(0, tm // MOVE_UNROLL, drain, 0)


def _dispatch(meta, pos, h2, n_rows):
    t, d = h2.shape
    tm = MOVE_ROWS
    grid_spec = pltpu.PrefetchScalarGridSpec(
        num_scalar_prefetch=1,
        grid=(t // tm,),
        in_specs=[
            pl.BlockSpec((TOPK_EXPERTS, tm), lambda i, meta: (0, i), memory_space=pltpu.SMEM),
            pl.BlockSpec(memory_space=pl.ANY),
        ],
        out_specs=pl.BlockSpec(memory_space=pl.ANY),
        scratch_shapes=[pltpu.VMEM((EXPERT_ROWS, d), F32), pltpu.SemaphoreType.DMA(()),
                        pltpu.SemaphoreType.DMA(())],
    )
    return pl.pallas_call(
        _dispatch_kernel,
        grid_spec=grid_spec,
        out_shape=jax.ShapeDtypeStruct((n_rows, d), F32),
        compiler_params=_params("arbitrary"),
        name="dispatch",
    )(meta, pos, h2)


def _expert_kernel(be_ref, meta_ref, x_ref, wgu_ref, bgu_ref, wdn_ref, bdn_ref, y_ref, wgu_bf, wdn_bf):
    i = pl.program_id(0)
    ff = wdn_ref.shape[2]
    n_used = meta_ref[1, 0]

    @pl.when(i >= n_used)
    def _():
        y_ref[...] = jnp.zeros_like(y_ref)

    @pl.when(i < n_used)
    def _():
        prev = be_ref[0, jnp.maximum(i - 1, 0)]

        @pl.when((i == 0) | (be_ref[0, i] != prev))
        def _():
            wgu_bf[...] = wgu_ref[0, 0].astype(BF16)
            wdn_bf[...] = wdn_ref[0, 0].astype(BF16)

        gu = _dot(x_ref[...].astype(BF16), wgu_bf[...]) + bgu_ref[0, 0]
        gate = jnp.minimum(gu[:, :ff], SWIGLU_LIMIT)
        up = jnp.clip(gu[:, ff:], -SWIGLU_LIMIT, SWIGLU_LIMIT)
        act = (up + 1.0) * gate * jax.nn.sigmoid(SWIGLU_ALPHA * gate)
        y_ref[...] = _dot(act.astype(BF16), wdn_bf[...]) + bdn_ref[0, 0]


def _experts(layer, blk_e, meta, xr, w_gu, b_gu, w_dn, b_dn):
    n_rows, d = xr.shape
    ff2 = w_gu.shape[-1]
    ff = ff2 // 2
    tm = EXPERT_ROWS
    nblk = n_rows // tm

    def last_used(i, meta):
        return jnp.minimum(i, meta[1, 0] - 1)

    xmap = lambda i, be, meta: (last_used(i, meta), 0)
    wmap = lambda i, be, meta: (layer, be[0, last_used(i, meta)], 0, 0)
    grid_spec = pltpu.PrefetchScalarGridSpec(
        num_scalar_prefetch=2,
        grid=(nblk,),
        in_specs=[
            pl.BlockSpec((tm, d), xmap),
            pl.BlockSpec((1, 1, d, ff2), wmap),
            pl.BlockSpec((1, 1, 1, ff2), wmap),
            pl.BlockSpec((1, 1, ff, d), wmap),
            pl.BlockSpec((1, 1, 1, d), wmap),
        ],
        out_specs=pl.BlockSpec((tm, d), lambda i, be, meta: (i, 0)),
        scratch_shapes=[pltpu.VMEM((d, ff2), BF16), pltpu.VMEM((ff, d), BF16)],
    )
    return pl.pallas_call(
        _expert_kernel,
        grid_spec=grid_spec,
        out_shape=jax.ShapeDtypeStruct((n_rows, d), F32),
        compiler_params=_params("arbitrary"),
        name="experts",
    )(blk_e, meta, xr, w_gu, b_gu, w_dn, b_dn)


def _combine_kernel(pos_ref, nxt_ref, x1_ref, g_ref, g2_ref, fw_ref, yr_ref, o_ref, buf, sems, *, final_norm):
    i = pl.program_id(0)
    n = pl.num_programs(0)
    tm = x1_ref.shape[0]
    slot = i % 2

    def gather(table_ref, dst_slot):
        def issue(g, carry):
            for u in range(MOVE_UNROLL):
                t = g * MOVE_UNROLL + u
                for k in range(TOPK_EXPERTS):
                    _rows_copy(yr_ref, table_ref[k, t], buf.at[dst_slot, k], t,
                               sems.at[dst_slot]).start(priority=(u + k) % 2)
            return carry
        lax.fori_loop(0, tm // MOVE_UNROLL, issue, 0)

    def drain(g, carry):
        for u in range(MOVE_UNROLL * TOPK_EXPERTS):
            _rows_copy(yr_ref, 0, buf.at[slot, 0], 0, sems.at[slot]).wait()
        return carry

    @pl.when(i == 0)
    def _():
        gather(pos_ref, 0)

    @pl.when(i + 1 < n)
    def _():
        gather(nxt_ref, 1 - slot)

    lax.fori_loop(0, tm // MOVE_UNROLL, drain, 0)
    acc = g_ref[:, 0:1] * buf[slot, 0]
    for k in range(1, TOPK_EXPERTS):
        acc = acc + g_ref[:, k:k + 1] * buf[slot, k]
    x2 = x1_ref[...] + g2_ref[0, 0] * acc
    if final_norm:
        ms = jnp.mean(x2 * x2, axis=-1, keepdims=True)
        x2 = x2 * lax.rsqrt(ms + NORM_EPS) * fw_ref[...]
    o_ref[...] = x2


def _combine(layer, pos, x1, gates, mod, final_w, yr, seq, final_norm):
    t, d = x1.shape
    tm = MOVE_ROWS
    per_b = seq // tm
    kk = TOPK_EXPERTS
    n = t // tm
    return pl.pallas_call(
        functools.partial(_combine_kernel, final_norm=final_norm),
        grid=(n,),
        in_specs=[
            pl.BlockSpec((kk, tm), lambda i: (0, i), memory_space=pltpu.SMEM),
            pl.BlockSpec((kk, tm), lambda i: (0, jnp.minimum(i + 1, n - 1)), memory_space=pltpu.SMEM),
            pl.BlockSpec((tm, d), lambda i: (i, 0)),
            pl.BlockSpec((tm, kk), lambda i: (i, 0)),
            _mod_spec(layer, 5, d, per_b),
            pl.BlockSpec((1, d), lambda i: (0, 0)),
            pl.BlockSpec(memory_space=pl.ANY),
        ],
        out_specs=pl.BlockSpec((tm, d), lambda i: (i, 0)),
        out_shape=jax.ShapeDtypeStruct((t, d), F32),
        scratch_shapes=[pltpu.VMEM((2, kk, tm, d), F32), pltpu.SemaphoreType.DMA((2,))],
        compiler_params=_params("arbitrary"),
        name="combine",
    )(pos, pos, x1, gates, mod, final_w, yr)


def kernel(x, c, ada_w, ada_b, norm_mix_w, w_in, ret_norm_w, lru_conv_w, lru_conv_b, lru_gate_a_w,
           lru_gate_a_b, lru_gate_x_w, lru_gate_x_b, lru_lambda, w_out, norm_ffn_w, router_w, router_b,
           moe_w_gu, moe_b_gu, moe_w_down, moe_b_down, final_norm_w):
    batch, seq, d = x.shape
    depth = ada_w.shape[0]
    t = batch * seq
    ne = N_EXPERTS
    n_blocks = (t * TOPK_EXPERTS) // EXPERT_ROWS + ne
    tables = _retention_tables(seq)
    mod = _ada_mod(c, ada_w, ada_b).reshape(depth, batch, 1, N_MOD * d)
    w_in_bf = w_in.astype(BF16)
    w_out_bf = w_out.astype(BF16)
    wa_bd = _block_diag(lru_gate_a_w).astype(BF16)
    wx_bd = _block_diag(lru_gate_x_w).astype(BF16)
    rw_t = jnp.swapaxes(router_w, 1, 2)
    vec = lambda p: p.reshape(depth, 1, p.shape[-1])
    b_gu = moe_b_gu.reshape(depth, ne, 1, -1)
    b_dn = moe_b_down.reshape(depth, ne, 1, d)
    xf = x.reshape(t, d)
    for l in range(depth):
        ret, moba, lru = _inproj(l, xf, vec(norm_mix_w), mod, w_in_bf, seq)
        y_ret = _retention(l, ret, vec(ret_norm_w), tables, batch, seq)
        y_moba = _moba(moba, batch, seq)
        y_lru = _lru(l, lru, lru_conv_w, vec(lru_conv_b), wa_bd, vec(lru_gate_a_b), wx_bd,
                     vec(lru_gate_x_b), vec(lru_lambda), batch, seq)
        x1, h2, top_i, gates, rank, counts = _outproj_router(
            l, xf, y_ret, y_moba, y_lru, w_out_bf, mod, vec(norm_ffn_w), rw_t,
            router_b.reshape(depth, ne, 1), seq)
        pos, blk_e, meta = _plan(top_i, rank, counts, n_blocks)
        xr = _dispatch(meta, pos, h2, n_blocks * EXPERT_ROWS)
        yr = _experts(l, blk_e, meta, xr, moe_w_gu, b_gu, moe_w_down, b_dn)
        xf = _combine(l, pos, x1, gates.T, mod, final_norm_w.reshape(1, d), yr, seq,
                      final_norm=(l == depth - 1))
    return xf.reshape(batch, seq, d)
```

```python
import functools

import jax
import jax.numpy as jnp
from jax import lax
from jax.experimental import pallas as pl
from jax.experimental.pallas import tpu as pltpu

F32 = jnp.float32
BF16 = jnp.bfloat16
I32 = jnp.int32

HEAD_DIM = 64
RET_HEADS = 4
RET_W = RET_HEADS * HEAD_DIM
RET_CHUNK = 128
ROPE_BASE = 10000.0
MOBA_HEADS = 6
MOBA_W = MOBA_HEADS * HEAD_DIM
MOBA_BLOCK = 256
MOBA_TOPK = 3
LRU_BLOCKS = 6
LRU_BLOCK_W = 64
LRU_W = LRU_BLOCKS * LRU_BLOCK_W
CONV_WIDTH = 4
LRU_C = 8.0
N_EXPERTS = 32
TOPK_EXPERTS = 4
SWIGLU_LIMIT = 7.0
SWIGLU_ALPHA = 1.702
NORM_EPS = 1e-6
N_MOD = 6

LANES = 128
VMEM_LIMIT = 56 * 1024 * 1024
ROW_TILE = 512
EXPERT_ROWS = 512
MOVE_ROWS = 256
MOVE_UNROLL = 8
PLAN_ROWS = 2048

NT_DIMS = (((1,), (1,)), ((), ()))
NN_DIMS = (((1,), (0,)), ((), ()))
TN_DIMS = (((0,), (0,)), ((), ()))


def _params(*semantics):
    return pltpu.CompilerParams(dimension_semantics=semantics, vmem_limit_bytes=VMEM_LIMIT)


def _dot(a, b, dims=NN_DIMS):
    return lax.dot_general(a, b, dims, preferred_element_type=F32)


def _split_bf16(a):
    hi = a.astype(BF16)
    lo = (a - hi.astype(F32)).astype(BF16)
    return hi, lo


def _dot3(a, b, dims=NN_DIMS):
    ah, al = _split_bf16(a)
    bh, bl = _split_bf16(b)
    return _dot(ah, bl, dims) + _dot(al, bh, dims) + _dot(ah, bh, dims)


def _rms_mod(x, nw, scale, shift):
    ms = jnp.mean(x * x, axis=-1, keepdims=True)
    y = x * lax.rsqrt(ms + NORM_EPS) * nw
    return y * (1.0 + scale) + shift


def _mod_spec(layer, chunk, d, per_b):
    return pl.BlockSpec((1, 1, 1, d), lambda i: (layer, i // per_b, 0, chunk))


def _layer_spec(layer, shape):
    zeros = (0,) * len(shape)
    return pl.BlockSpec((1,) + tuple(shape), lambda *_: (layer,) + zeros)


def _ada_kernel(c_ref, w_ref, b_ref, o_ref):
    c = c_ref[...]
    c_act = c * jax.nn.sigmoid(c)
    o_ref[0] = _dot3(c_act, w_ref[0]) + b_ref[0]


def _ada_mod(c, ada_w, ada_b):
    depth, d, n = ada_w.shape
    b = c.shape[0]
    tn = n // 4
    return pl.pallas_call(
        _ada_kernel,
        grid=(depth, n // tn),
        in_specs=[
            pl.BlockSpec((b, d), lambda l, j: (0, 0)),
            pl.BlockSpec((1, d, tn), lambda l, j: (l, 0, j)),
            pl.BlockSpec((1, 1, tn), lambda l, j: (l, 0, j)),
        ],
        out_specs=pl.BlockSpec((1, b, tn), lambda l, j: (l, 0, j)),
        out_shape=jax.ShapeDtypeStruct((depth, b, n), F32),
        compiler_params=_params("arbitrary", "arbitrary"),
        name="ada_mod",
    )(c, ada_w, ada_b.reshape(depth, 1, n))


def _inproj_kernel(x_ref, nw_ref, sc_ref, sh_ref, w_ref, ret_ref, moba_ref, lru_ref):
    h = _rms_mod(x_ref[...], nw_ref[0], sc_ref[0, 0], sh_ref[0, 0]).astype(BF16)
    o0 = RET_W * 4
    o1 = o0 + MOBA_W * 3
    ret_ref[...] = _dot(h, w_ref[0, :, 0:o0])
    moba_ref[...] = _dot(h, w_ref[0, :, o0:o1])
    lru_ref[...] = _dot(h, w_ref[0, :, o1:])


def _inproj(layer, x2d, norm_w, mod, w_bf16, seq):
    t, d = x2d.shape
    tm = ROW_TILE
    per_b = seq // tm
    n_ret, n_moba, n_lru = RET_W * 4, MOBA_W * 3, LRU_W * 2
    return pl.pallas_call(
        _inproj_kernel,
        grid=(t // tm,),
        in_specs=[
            pl.BlockSpec((tm, d), lambda i: (i, 0)),
            _layer_spec(layer, (1, d)),
            _mod_spec(layer, 1, d, per_b),
            _mod_spec(layer, 0, d, per_b),
            _layer_spec(layer, w_bf16.shape[1:]),
        ],
        out_specs=[
            pl.BlockSpec((tm, n_ret), lambda i: (i, 0)),
            pl.BlockSpec((tm, n_moba), lambda i: (i, 0)),
            pl.BlockSpec((tm, n_lru), lambda i: (i, 0)),
        ],
        out_shape=[
            jax.ShapeDtypeStruct((t, n_ret), F32),
            jax.ShapeDtypeStruct((t, n_moba), F32),
            jax.ShapeDtypeStruct((t, n_lru), F32),
        ],
        compiler_params=_params("arbitrary"),
        name="inproj",
    )(x2d, norm_w, mod, mod, w_bf16)


def _retention_tables(seq):
    h, dh, c = RET_HEADS, HEAD_DIM, RET_CHUNK
    half = dh // 2
    inv = 1.0 / (ROPE_BASE ** (jnp.arange(half, dtype=F32) / half))
    ang = jnp.arange(seq).astype(F32)[:, None] * inv[None, :]
    cos, sin = jnp.cos(ang), jnp.sin(ang)
    cos_t = jnp.tile(jnp.concatenate([cos, cos], axis=-1), (1, h))
    sin_t = jnp.tile(jnp.concatenate([-sin, sin], axis=-1), (1, h))
    log_g = jnp.log1p(-jnp.exp2(-5.0 - jnp.arange(h, dtype=F32)))
    idx = jnp.arange(c, dtype=F32)
    diff = idx[:, None] - idx[None, :]
    dmat = jnp.where(diff >= 0, jnp.exp(log_g[:, None, None] * jnp.maximum(diff, 0.0)), 0.0)
    zeta = jnp.exp(log_g[:, None] * (c - 1 - idx)[None, :])
    xi = jnp.exp(log_g[:, None] * (idx + 1)[None, :])
    zeta_t = jnp.repeat(zeta.T, dh, axis=1)
    xi_t = jnp.repeat(xi.T, dh, axis=1)
    decay = jnp.broadcast_to(jnp.exp(log_g * c)[:, None, None], (h, dh, dh))
    return cos_t, sin_t, dmat, zeta_t, xi_t, decay


def _retention_kernel(p_ref, cos_ref, sin_ref, dmat_ref, zeta_ref, xi_ref, dec_ref, nw_ref, o_ref):
    seq = p_ref.shape[0]
    c, dh, w = RET_CHUNK, HEAD_DIM, RET_W
    lane = lax.broadcasted_iota(I32, (1, w), 1)
    first_half = (lane % dh) < (dh // 2)

    def rope(x, cos, sin):
        partner = jnp.where(first_half, pltpu.roll(x, w - dh // 2, 1), pltpu.roll(x, dh // 2, 1))
        return x * cos + partner * sin

    def chunk(n, states):
        r0 = pl.multiple_of(n * c, c)
        rows = pl.ds(r0, c)
        cos, sin = cos_ref[rows, :], sin_ref[rows, :]
        q = rope(p_ref[rows, 0:w], cos, sin)
        k = rope(p_ref[rows, w:2 * w], cos, sin) * (dh ** -0.5)
        v = p_ref[rows, 2 * w:3 * w]
        g = p_ref[rows, 3 * w:4 * w]
        kz = k * zeta_ref[...]
        gate = g * jax.nn.sigmoid(g) * nw_ref[0]
        new_states = []
        for hd in range(RET_HEADS):
            cols = slice(hd * dh, (hd + 1) * dh)
            qh = q[:, cols].astype(BF16)
            kh = k[:, cols].astype(BF16)
            vh = v[:, cols].astype(BF16)
            scores = _dot(qh, kh, NT_DIMS) * dmat_ref[hd]
            intra = _dot(scores.astype(BF16), vh)
            cross = _dot(qh, states[hd].astype(BF16)) * xi_ref[:, cols]
            kv = _dot(kz[:, cols].astype(BF16), vh, TN_DIMS)
            y = intra + cross
            mu = jnp.mean(y, axis=-1, keepdims=True)
            yc = y - mu
            var = jnp.mean(yc * yc, axis=-1, keepdims=True)
            o_ref[rows, cols] = yc * lax.rsqrt(var + NORM_EPS) * gate[:, cols]
            new_states.append(states[hd] * dec_ref[hd] + kv)
        return tuple(new_states)

    init = tuple(jnp.zeros((dh, dh), F32) for _ in range(RET_HEADS))
    lax.fori_loop(0, seq // c, chunk, init)


def _retention(layer, ret, norm_w, tables, batch, seq):
    cos_t, sin_t, dmat, zeta_t, xi_t, decay = tables
    w = RET_W
    const2 = lambda b: (0, 0)
    const3 = lambda b: (0, 0, 0)
    return pl.pallas_call(
        _retention_kernel,
        grid=(batch,),
        in_specs=[
            pl.BlockSpec((seq, 4 * w), lambda b: (b, 0)),
            pl.BlockSpec((seq, w), const2),
            pl.BlockSpec((seq, w), const2),
            pl.BlockSpec(dmat.shape, const3),
            pl.BlockSpec(zeta_t.shape, const2),
            pl.BlockSpec(xi_t.shape, const2),
            pl.BlockSpec(decay.shape, const3),
            _layer_spec(layer, (1, w)),
        ],
        out_specs=pl.BlockSpec((seq, w), lambda b: (b, 0)),
        out_shape=jax.ShapeDtypeStruct((batch * seq, w), F32),
        compiler_params=_params("arbitrary"),
        name="retention",
    )(ret, cos_t, sin_t, dmat, zeta_t, xi_t, decay, norm_w)


def _moba_kernel(q_ref, k_ref, v_ref, o_ref):
    seq = q_ref.shape[0]
    bk, dh = MOBA_BLOCK, HEAD_DIM
    nb = seq // bk
    scale = dh ** -0.5
    neg = -jnp.inf
    lane = lax.broadcasted_iota(I32, (1, LANES), 1)
    k_all = k_ref[...]
    k_bf = k_all.astype(BF16)
    v_bf = v_ref[...].astype(BF16)
    kmean = jnp.mean(k_all.reshape(nb, bk, LANES), axis=1)
    colid = lax.broadcasted_iota(I32, (bk, nb), 1)
    causal = (lax.broadcasted_iota(I32, (bk, bk), 0)
              >= lax.broadcasted_iota(I32, (bk, bk), 1))
    heads_per_step = LANES // dh
    outs = [None] * nb
    for p in range(heads_per_step):
        head_lanes = (lane >= p * dh) & (lane < (p + 1) * dh)
        qm = jnp.where(head_lanes, q_ref[...], 0.0)
        gs = _dot3(qm, kmean, NT_DIMS)
        qm_bf = qm.astype(BF16)
        for i in range(nb):
            rows = slice(i * bk, (i + 1) * bk)
            klen = (i + 1) * bk
            s = _dot(qm_bf[rows], k_bf[:klen], NT_DIMS) * scale
            pieces = []
            if i > MOBA_TOPK:
                gsi = gs[rows]
                rank = jnp.zeros((bk, nb), F32)
                for j2 in range(i):
                    cj = gsi[:, j2:j2 + 1]
                    tie = jnp.where(colid > j2, 1.0, 0.0)
                    rank = rank + jnp.where(cj > gsi, 1.0, jnp.where(cj == gsi, tie, 0.0))
                for j in range(i):
                    keep = rank[:, j:j + 1] < float(MOBA_TOPK)
                    pieces.append(jnp.where(keep, s[:, j * bk:(j + 1) * bk], neg))
            else:
                for j in range(i):
                    pieces.append(s[:, j * bk:(j + 1) * bk])
            pieces.append(jnp.where(causal, s[:, i * bk:klen], neg))
            m = pieces[0]
            for pc in pieces[1:]:
                m = jnp.maximum(m, pc)
            m = jnp.max(m, axis=-1, keepdims=True)
            probs = [jnp.exp(pc - m) for pc in pieces]
            tot = probs[0]
            for pr in probs[1:]:
                tot = tot + pr
            denom = jnp.sum(tot, axis=-1, keepdims=True)
            pcat = probs[0] if len(probs) == 1 else jnp.concatenate(probs, axis=1)
            o = _dot(pcat.astype(BF16), v_bf[:klen]) / denom
            outs[i] = o if outs[i] is None else jnp.where(head_lanes, o, outs[i])
    for i in range(nb):
        o_ref[i * bk:(i + 1) * bk, :] = outs[i]


def _moba(moba, batch, seq):
    npair = MOBA_W // LANES
    return pl.pallas_call(
        _moba_kernel,
        grid=(batch, npair),
        in_specs=[
            pl.BlockSpec((seq, LANES), lambda b, p: (b, p)),
            pl.BlockSpec((seq, LANES), lambda b, p: (b, npair + p)),
            pl.BlockSpec((seq, LANES), lambda b, p: (b, 2 * npair + p)),
        ],
        out_specs=pl.BlockSpec((seq, LANES), lambda b, p: (b, p)),
        out_shape=jax.ShapeDtypeStruct((batch * seq, MOBA_W), F32),
        compiler_params=_params("arbitrary", "arbitrary"),
        name="moba",
    )(moba, moba, moba)


def _lru_kernel(xg_ref, cw_ref, cb_ref, wa_ref, ba_ref, wx_ref, bx_ref, lam_ref, o_ref):
    seq = xg_ref.shape[0]
    w = LRU_W
    xb = xg_ref[:, 0:w]
    row = lax.broadcasted_iota(I32, (seq, w), 0)
    u = cb_ref[0] + cw_ref[0, CONV_WIDTH - 1:CONV_WIDTH, :] * xb
    for tap in range(CONV_WIDTH - 1):
        back = CONV_WIDTH - 1 - tap
        u = u + cw_ref[0, tap:tap + 1, :] * jnp.where(row >= back, pltpu.roll(xb, back, 0), 0.0)
    u_bf = u.astype(BF16)
    r = jax.nn.sigmoid(_dot(u_bf, wa_ref[0]) + ba_ref[0])
    ig = jax.nn.sigmoid(_dot(u_bf, wx_ref[0]) + bx_ref[0])
    z = -lam_ref[0]
    softplus = jnp.maximum(z, 0.0) + jnp.log1p(jnp.exp(-jnp.abs(z)))
    log_a = -LRU_C * r * softplus
    a = jnp.exp(log_a)
    b = jnp.sqrt(-jnp.tanh(log_a) * (a * a + 1.0)) * (ig * u)
    d = 1
    while d < seq:
        live = row >= d
        a_prev = jnp.where(live, pltpu.roll(a, d, 0), 1.0)
        b_prev = jnp.where(live, pltpu.roll(b, d, 0), 0.0)
        b = a * b_prev + b
        a = a * a_prev
        d *= 2
    o_ref[...] = jax.nn.gelu(xg_ref[:, w:2 * w]) * b


def _block_diag(wb):
    depth, g, n, _ = wb.shape
    eye = jnp.eye(g, dtype=wb.dtype)
    return (eye[None, :, None, :, None] * wb[:, :, :, None, :]).reshape(depth, g * n, g * n)


def _lru(layer, lru, cw, cb, wa_bd, ba, wx_bd, bx, lam, batch, seq):
    w = LRU_W
    vec = _layer_spec(layer, (1, w))
    mat = _layer_spec(layer, (w, w))
    return pl.pallas_call(
        _lru_kernel,
        grid=(batch,),
        in_specs=[
            pl.BlockSpec((seq, 2 * w), lambda b: (b, 0)),
            _layer_spec(layer, (CONV_WIDTH, w)),
            vec, mat, vec, mat, vec, vec,
        ],
        out_specs=pl.BlockSpec((seq, w), lambda b: (b, 0)),
        out_shape=jax.ShapeDtypeStruct((batch * seq, w), F32),
        compiler_params=_params("arbitrary"),
        name="rg_lru",
    )(lru, cw, cb, wa_bd, ba, wx_bd, bx, lam)


def _outproj_router_kernel(x_ref, yr_ref, ym_ref, yl_ref, wo_ref, g1_ref, nw_ref,
                           sc_ref, sh_ref, rw_ref, rb_ref,
                           x1_ref, h2_ref, ti_ref, tg_ref, rk_ref, cnt_ref, carry_ref):
    i = pl.program_id(0)
    tm = x_ref.shape[0]
    ne = N_EXPERTS
    o0, o1 = RET_W, RET_W + MOBA_W

    @pl.when(i == 0)
    def _():
        carry_ref[...] = jnp.zeros_like(carry_ref)

    mixed = (_dot(yr_ref[...].astype(BF16), wo_ref[0, 0:o0, :])
             + _dot(ym_ref[...].astype(BF16), wo_ref[0, o0:o1, :])
             + _dot(yl_ref[...].astype(BF16), wo_ref[0, o1:, :]))
    x1 = x_ref[...] + g1_ref[0, 0] * mixed
    x1_ref[...] = x1
    h2 = _rms_mod(x1, nw_ref[0], sc_ref[0, 0], sh_ref[0, 0])
    h2_ref[...] = h2

    logits = _dot3(rw_ref[0], h2, NT_DIMS) + rb_ref[0]
    eid = lax.broadcasted_iota(I32, (ne, tm), 0)
    work = logits
    member = jnp.zeros((ne, tm), F32)
    vals, hots = [], []
    for k in range(TOPK_EXPERTS):
        m = jnp.max(work, axis=0, keepdims=True)
        idx = jnp.min(jnp.where(work == m, eid, ne), axis=0, keepdims=True)
        hot = eid == idx
        ti_ref[k:k + 1, :] = idx
        vals.append(m)
        hots.append(hot)
        member = member + jnp.where(hot, 1.0, 0.0)
        work = jnp.where(hot, -jnp.inf, work)
    exps = [jnp.exp(v - vals[0]) for v in vals]
    tot = exps[0]
    for e in exps[1:]:
        tot = tot + e
    for k in range(TOPK_EXPERTS):
        tg_ref[k:k + 1, :] = exps[k] / tot

    before = (lax.broadcasted_iota(I32, (tm, tm), 0)
              < lax.broadcasted_iota(I32, (tm, tm), 1))
    prefix = _dot(member.astype(BF16), jnp.where(before, 1.0, 0.0).astype(BF16))
    prefix = prefix + carry_ref[:, 0:1]
    for k in range(TOPK_EXPERTS):
        rk = jnp.sum(jnp.where(hots[k], prefix, 0.0), axis=0, keepdims=True)
        rk_ref[k:k + 1, :] = rk.astype(I32)
    carry_ref[...] = carry_ref[...] + jnp.sum(member, axis=1, keepdims=True)
    cnt_ref[...] = carry_ref[...].astype(I32)


def _outproj_router(layer, x2d, y_ret, y_moba, y_lru, wo_bf16, mod, norm_w, rw_t, rb, seq):
    t, d = x2d.shape
    tm = ROW_TILE
    per_b = seq // tm
    ne, kk = N_EXPERTS, TOPK_EXPERTS
    const = lambda i: (0, 0)
    row = lambda n: pl.BlockSpec((tm, n), lambda i: (i, 0))
    lanes = pl.BlockSpec((kk, tm), lambda i: (0, i))
    return pl.pallas_call(
        _outproj_router_kernel,
        grid=(t // tm,),
        in_specs=[
            row(d), row(RET_W), row(MOBA_W), row(LRU_W),
            _layer_spec(layer, wo_bf16.shape[1:]),
            _mod_spec(layer, 2, d, per_b),
            _layer_spec(layer, (1, d)),
            _mod_spec(layer, 4, d, per_b),
            _mod_spec(layer, 3, d, per_b),
            _layer_spec(layer, (ne, d)),
            _layer_spec(layer, (ne, 1)),
        ],
        out_specs=[
            row(d), row(d), lanes, lanes, lanes,
            pl.BlockSpec((ne, LANES), const),
        ],
        out_shape=[
            jax.ShapeDtypeStruct((t, d), F32),
            jax.ShapeDtypeStruct((t, d), F32),
            jax.ShapeDtypeStruct((kk, t), I32),
            jax.ShapeDtypeStruct((kk, t), F32),
            jax.ShapeDtypeStruct((kk, t), I32),
            jax.ShapeDtypeStruct((ne, LANES), I32),
        ],
        scratch_shapes=[pltpu.VMEM((ne, LANES), F32)],
        compiler_params=_params("arbitrary"),
        name="outproj_router",
    )(x2d, y_ret, y_moba, y_lru, wo_bf16, mod, norm_w, mod, mod, rw_t, rb)


def _plan_kernel(ti_ref, rk_ref, cnt_ref, pos_ref, blk_ref, meta_ref):
    ne = N_EXPERTS
    tm = ti_ref.shape[1]
    nblk_pad = blk_ref.shape[1]
    shift = EXPERT_ROWS.bit_length() - 1
    counts = cnt_ref[...]
    padded = ((counts + (EXPERT_ROWS - 1)) >> shift) << shift
    row = lax.broadcasted_iota(I32, (ne, LANES), 0)
    pend = padded
    d = 1
    while d < ne:
        pend = pend + jnp.where(row >= d, pltpu.roll(pend, d, 0), 0)
        d *= 2
    pstart = (pend - padded)[:, 0:1]
    eid = lax.broadcasted_iota(I32, (ne, tm), 0)
    for k in range(TOPK_EXPERTS):
        base = jnp.sum(jnp.where(ti_ref[k:k + 1, :] == eid, pstart, 0), axis=0, keepdims=True)
        pos_ref[k:k + 1, :] = base + rk_ref[k:k + 1, :]
    starts = lax.broadcasted_iota(I32, (ne, nblk_pad), 1) * EXPERT_ROWS
    owner = jnp.sum(jnp.where(pend[:, 0:1] <= starts, 1, 0), axis=0, keepdims=True)
    owner = jnp.minimum(owner, ne - 1)
    blk_ref[0:1, :] = owner
    eid_blk = lax.broadcasted_iota(I32, (ne, nblk_pad), 0)
    later = (eid_blk > owner) & (padded[:, 0:1] > 0)
    blk_ref[1:2, :] = jnp.min(jnp.where(later, eid_blk, ne), axis=0, keepdims=True)
    lane = lax.broadcasted_iota(I32, (ne, LANES), 1)
    pend_lanes = jnp.sum(jnp.where(row == lane, pend, 0), axis=0, keepdims=True)
    total = jnp.max(pend, axis=0, keepdims=True)
    meta_ref[0:1, :] = pend_lanes
    meta_ref[1:2, :] = total >> shift
    meta_ref[2:3, :] = jnp.sum(jnp.where(row == lane, padded, 0), axis=0, keepdims=True)
    meta_ref[3:8, :] = jnp.zeros((5, LANES), I32)


def _plan(top_i, rank, counts, n_blocks):
    kk, t = top_i.shape
    tm = PLAN_ROWS
    nblk_pad = -(-n_blocks // LANES) * LANES
    const = lambda i: (0, 0)
    return pl.pallas_call(
        _plan_kernel,
        grid=(t // tm,),
        in_specs=[
            pl.BlockSpec((kk, tm), lambda i: (0, i)),
            pl.BlockSpec((kk, tm), lambda i: (0, i)),
            pl.BlockSpec(counts.shape, const),
        ],
        out_specs=[
            pl.BlockSpec((kk, tm), lambda i: (0, i)),
            pl.BlockSpec((2, nblk_pad), const),
            pl.BlockSpec((8, LANES), const),
        ],
        out_shape=[
            jax.ShapeDtypeStruct((kk, t), I32),
            jax.ShapeDtypeStruct((2, nblk_pad), I32),
            jax.ShapeDtypeStruct((8, LANES), I32),
        ],
        compiler_params=_params("arbitrary"),
        name="plan",
    )(top_i, rank, counts)


def _rows_copy(src_ref, src_row, dst_ref, dst_row, sem, n=1):
    return pltpu.make_async_copy(src_ref.at[pl.ds(src_row, n)], dst_ref.at[pl.ds(dst_row, n)], sem)


def _dispatch_kernel(meta_ref, pos_ref, h_ref, xr_ref, zeros_ref, hbuf, fill_sem, row_sem, load_sems):
    i = pl.program_id(0)
    last = pl.num_programs(0) - 1
    tm = pos_ref.shape[1]
    blk = EXPERT_ROWS
    n_blocks = xr_ref.shape[0] // blk

    def fill(dst_row):
        return _rows_copy(zeros_ref, 0, xr_ref, pl.multiple_of(dst_row, blk), fill_sem, blk)

    def owns_rows(e):
        return meta_ref[2, e] > 0

    @pl.when(i == 0)
    def _():
        zeros_ref[...] = jnp.zeros_like(zeros_ref)
        for e in range(N_EXPERTS):
            @pl.when(owns_rows(e))
            def _():
                fill(meta_ref[0, e] - blk).start()

        def tail_start(b, carry):
            fill(b * blk).start()
            return carry

        def tail_wait(b, carry):
            fill(b * blk).wait()
            return carry

        lax.fori_loop(meta_ref[1, 0], n_blocks, tail_start, 0)
        for e in range(N_EXPERTS):
            @pl.when(owns_rows(e))
            def _():
                fill(meta_ref[0, e] - blk).wait()
        lax.fori_loop(meta_ref[1, 0], n_blocks, tail_wait, 0)

    slot = i % 2

    def load(step, dst_slot):
        src_row = pl.multiple_of(step * tm, tm)
        return pltpu.make_async_copy(h_ref.at[pl.ds(src_row, tm)], hbuf.at[dst_slot], load_sems.at[dst_slot])

    def issue(g, carry):
        for u in range(MOVE_UNROLL):
            t = g * MOVE_UNROLL + u
            for k in range(TOPK_EXPERTS):
                _rows_copy(hbuf.at[slot], t, xr_ref, pos_ref[k, t], row_sem).start(priority=(u + k) % 2)
        return carry

    def drain(g, carry):
        for u in range(MOVE_UNROLL * TOPK_EXPERTS):
            _rows_copy(hbuf.at[0], 0, xr_ref, 0, row_sem).wait()
        return carry

    @pl.when(i == 0)
    def _():
        load(0, 0).start()

    @pl.when(i > 0)
    def _():
        lax.fori_loop(0, tm // MOVE_UNROLL, drain, 0)

    @pl.when(i < last)
    def _():
        load(i + 1, 1 - slot).start()

    load(i, slot).wait()
    lax.fori_loop(0, tm // MOVE_UNROLL, issue, 0)

    @pl.when(i == last)
    def _():
        lax.fori_loop(0, tm // MOVE_UNROLL, drain, 0)


def _dispatch(meta, pos, h2, n_rows):
    t, d = h2.shape
    tm = MOVE_ROWS
    grid_spec = pltpu.PrefetchScalarGridSpec(
        num_scalar_prefetch=1,
        grid=(t // tm,),
        in_specs=[
            pl.BlockSpec((TOPK_EXPERTS, tm), lambda i, meta: (0, i), memory_space=pltpu.SMEM),
            pl.BlockSpec(memory_space=pl.ANY),
        ],
        out_specs=pl.BlockSpec(memory_space=pl.ANY),
        scratch_shapes=[pltpu.VMEM((EXPERT_ROWS, d), F32), pltpu.VMEM((2, tm, d), F32),
                        pltpu.SemaphoreType.DMA(()), pltpu.SemaphoreType.DMA(()),
                        pltpu.SemaphoreType.DMA((2,))],
    )
    return pl.pallas_call(
        _dispatch_kernel,
        grid_spec=grid_spec,
        out_shape=jax.ShapeDtypeStruct((n_rows, d), F32),
        compiler_params=_params("arbitrary"),
        name="dispatch",
    )(meta, pos, h2)


def _expert_kernel(be_ref, meta_ref, x_ref, wgu_ref, bgu_ref, wdn_ref, bdn_ref, y_ref,
                   wgu_stage, wdn_stage, wgu_bf, wdn_bf, sems, *, layer):
    i = pl.program_id(0)
    ff = wdn_bf.shape[0]
    n_used = meta_ref[1, 0]

    def fetch(e):
        return (pltpu.make_async_copy(wgu_ref.at[layer, e], wgu_stage, sems.at[0]),
                pltpu.make_async_copy(wdn_ref.at[layer, e], wdn_stage, sems.at[1]))

    @pl.when(i >= n_used)
    def _():
        y_ref[...] = jnp.zeros_like(y_ref)

    @pl.when(i < n_used)
    def _():
        cur = be_ref[0, i]
        prev = be_ref[0, jnp.maximum(i - 1, 0)]

        @pl.when(i == 0)
        def _():
            for cp in fetch(cur):
                cp.start()

        @pl.when((i == 0) | (cur != prev))
        def _():
            for cp in fetch(cur):
                cp.wait()
            wgu_bf[...] = wgu_stage[...].astype(BF16)
            wdn_bf[...] = wdn_stage[...].astype(BF16)
            nxt = be_ref[1, i]

            @pl.when(nxt < N_EXPERTS)
            def _():
                for cp in fetch(nxt):
                    cp.start()

        gu = _dot(x_ref[...].astype(BF16), wgu_bf[...]) + bgu_ref[0, 0]
        gate = jnp.minimum(gu[:, :ff], SWIGLU_LIMIT)
        up = jnp.clip(gu[:, ff:], -SWIGLU_LIMIT, SWIGLU_LIMIT)
        act = (up + 1.0) * gate * jax.nn.sigmoid(SWIGLU_ALPHA * gate)
        y_ref[...] = _dot(act.astype(BF16), wdn_bf[...]) + bdn_ref[0, 0]


def _experts(layer, blk_e, meta, xr, w_gu, b_gu, w_dn, b_dn):
    n_rows, d = xr.shape
    ff2 = w_gu.shape[-1]
    ff = ff2 // 2
    tm = EXPERT_ROWS
    nblk = n_rows // tm

    def last_used(i, meta):
        return jnp.minimum(i, meta[1, 0] - 1)

    xmap = lambda i, be, meta: (last_used(i, meta), 0)
    wmap = lambda i, be, meta: (layer, be[0, last_used(i, meta)], 0, 0)
    grid_spec = pltpu.PrefetchScalarGridSpec(
        num_scalar_prefetch=2,
        grid=(nblk,),
        in_specs=[
            pl.BlockSpec((tm, d), xmap),
            pl.BlockSpec(memory_space=pl.ANY),
            pl.BlockSpec((1, 1, 1, ff2), wmap),
            pl.BlockSpec(memory_space=pl.ANY),
            pl.BlockSpec((1, 1, 1, d), wmap),
        ],
        out_specs=pl.BlockSpec((tm, d), lambda i, be, meta: (i, 0)),
        scratch_shapes=[pltpu.VMEM((d, ff2), F32), pltpu.VMEM((ff, d), F32),
                        pltpu.VMEM((d, ff2), BF16), pltpu.VMEM((ff, d), BF16),
                        pltpu.SemaphoreType.DMA((2,))],
    )
    return pl.pallas_call(
        functools.partial(_expert_kernel, layer=layer),
        grid_spec=grid_spec,
        out_shape=jax.ShapeDtypeStruct((n_rows, d), F32),
        compiler_params=_params("arbitrary"),
        name="experts",
    )(blk_e, meta, xr, w_gu, b_gu, w_dn, b_dn)


def _combine_kernel(pos_ref, nxt_ref, x1_ref, g_ref, g2_ref, fw_ref, yr_ref, o_ref, buf, sems, *, final_norm):
    i = pl.program_id(0)
    n = pl.num_programs(0)
    tm = x1_ref.shape[0]
    slot = i % 2

    def gather(table_ref, dst_slot):
        def issue(g, carry):
            for u in range(MOVE_UNROLL):
                t = g * MOVE_UNROLL + u
                for k in range(TOPK_EXPERTS):
                    _rows_copy(yr_ref, table_ref[k, t], buf.at[dst_slot, k], t,
                               sems.at[dst_slot]).start(priority=(u + k) % 2)
            return carry
        lax.fori_loop(0, tm // MOVE_UNROLL, issue, 0)

    def drain(g, carry):
        for u in range(MOVE_UNROLL * TOPK_EXPERTS):
            _rows_copy(yr_ref, 0, buf.at[slot, 0], 0, sems.at[slot]).wait()
        return carry

    @pl.when(i == 0)
    def _():
        gather(pos_ref, 0)

    @pl.when(i + 1 < n)
    def _():
        gather(nxt_ref, 1 - slot)

    lax.fori_loop(0, tm // MOVE_UNROLL, drain, 0)
    acc = g_ref[:, 0:1] * buf[slot, 0]
    for k in range(1, TOPK_EXPERTS):
        acc = acc + g_ref[:, k:k + 1] * buf[slot, k]
    x2 = x1_ref[...] + g2_ref[0, 0] * acc
    if final_norm:
        ms = jnp.mean(x2 * x2, axis=-1, keepdims=True)
        x2 = x2 * lax.rsqrt(ms + NORM_EPS) * fw_ref[...]
    o_ref[...] = x2


def _combine(layer, pos, x1, gates, mod, final_w, yr, seq, final_norm):
    t, d = x1.shape
    tm = MOVE_ROWS
    per_b = seq // tm
    kk = TOPK_EXPERTS
    n = t // tm
    return pl.pallas_call(
        functools.partial(_combine_kernel, final_norm=final_norm),
        grid=(n,),
        in_specs=[
            pl.BlockSpec((kk, tm), lambda i: (0, i), memory_space=pltpu.SMEM),
            pl.BlockSpec((kk, tm), lambda i: (0, jnp.minimum(i + 1, n - 1)), memory_space=pltpu.SMEM),
            pl.BlockSpec((tm, d), lambda i: (i, 0)),
            pl.BlockSpec((tm, kk), lambda i: (i, 0)),
            _mod_spec(layer, 5, d, per_b),
            pl.BlockSpec((1, d), lambda i: (0, 0)),
            pl.BlockSpec(memory_space=pl.ANY),
        ],
        out_specs=pl.BlockSpec((tm, d), lambda i: (i, 0)),
        out_shape=jax.ShapeDtypeStruct((t, d), F32),
        scratch_shapes=[pltpu.VMEM((2, kk, tm, d), F32), pltpu.SemaphoreType.DMA((2,))],
        compiler_params=_params("arbitrary"),
        name="combine",
    )(pos, pos, x1, gates, mod, final_w, yr)


def kernel(x, c, ada_w, ada_b, norm_mix_w, w_in, ret_norm_w, lru_conv_w, lru_conv_b, lru_gate_a_w,
           lru_gate_a_b, lru_gate_x_w, lru_gate_x_b, lru_lambda, w_out, norm_ffn_w, router_w, router_b,
           moe_w_gu, moe_b_gu, moe_w_down, moe_b_down, final_norm_w):
    batch, seq, d = x.shape
    depth = ada_w.shape[0]
    t = batch * seq
    ne = N_EXPERTS
    n_blocks = (t * TOPK_EXPERTS) // EXPERT_ROWS + ne
    tables = _retention_tables(seq)
    mod = _ada_mod(c, ada_w, ada_b).reshape(depth, batch, 1, N_MOD * d)
    w_in_bf = w_in.astype(BF16)
    w_out_bf = w_out.astype(BF16)
    wa_bd = _block_diag(lru_gate_a_w).astype(BF16)
    wx_bd = _block_diag(lru_gate_x_w).astype(BF16)
    rw_t = jnp.swapaxes(router_w, 1, 2)
    vec = lambda p: p.reshape(depth, 1, p.shape[-1])
    b_gu = moe_b_gu.reshape(depth, ne, 1, -1)
    b_dn = moe_b_down.reshape(depth, ne, 1, d)
    xf = x.reshape(t, d)
    for l in range(depth):
        ret, moba, lru = _inproj(l, xf, vec(norm_mix_w), mod, w_in_bf, seq)
        y_ret = _retention(l, ret, vec(ret_norm_w), tables, batch, seq)
        y_moba = _moba(moba, batch, seq)
        y_lru = _lru(l, lru, lru_conv_w, vec(lru_conv_b), wa_bd, vec(lru_gate_a_b), wx_bd,
                     vec(lru_gate_x_b), vec(lru_lambda), batch, seq)
        x1, h2, top_i, gates, rank, counts = _outproj_router(
            l, xf, y_ret, y_moba, y_lru, w_out_bf, mod, vec(norm_ffn_w), rw_t,
            router_b.reshape(depth, ne, 1), seq)
        pos, blk_e, meta = _plan(top_i, rank, counts, n_blocks)
        xr = _dispatch(meta, pos, h2, n_blocks * EXPERT_ROWS)
        yr = _experts(l, blk_e, meta, xr, moe_w_gu, b_gu, moe_w_down, b_dn)
        xf = _combine(l, pos, x1, gates.T, mod, final_norm_w.reshape(1, d), yr, seq,
                      final_norm=(l == depth - 1))
    return xf.reshape(batch, seq, d)
```

```python
import functools

import jax
import jax.numpy as jnp
from jax import lax
from jax.experimental import pallas as pl
from jax.experimental.pallas import tpu as pltpu

F32 = jnp.float32
BF16 = jnp.bfloat16
I32 = jnp.int32

HEAD_DIM = 64
RET_HEADS = 4
RET_W = RET_HEADS * HEAD_DIM
RET_CHUNK = 128
ROPE_BASE = 10000.0
MOBA_HEADS = 6
MOBA_W = MOBA_HEADS * HEAD_DIM
MOBA_BLOCK = 256
MOBA_TOPK = 3
LRU_BLOCKS = 6
LRU_BLOCK_W = 64
LRU_W = LRU_BLOCKS * LRU_BLOCK_W
CONV_WIDTH = 4
LRU_C = 8.0
N_EXPERTS = 32
TOPK_EXPERTS = 4
SWIGLU_LIMIT = 7.0
SWIGLU_ALPHA = 1.702
NORM_EPS = 1e-6
N_MOD = 6

LANES = 128
ROW_SLABS = 8
VMEM_LIMIT = 56 * 1024 * 1024
ROW_TILE = 512
EXPERT_ROWS = 512
MOVE_ROWS = 256
MOVE_UNROLL = 8
PLAN_ROWS = 2048

NT_DIMS = (((1,), (1,)), ((), ()))
NN_DIMS = (((1,), (0,)), ((), ()))
TN_DIMS = (((0,), (0,)), ((), ()))


def _params(*semantics):
    return pltpu.CompilerParams(dimension_semantics=semantics, vmem_limit_bytes=VMEM_LIMIT)


def _dot(a, b, dims=NN_DIMS):
    return lax.dot_general(a, b, dims, preferred_element_type=F32)


def _split_bf16(a):
    hi = a.astype(BF16)
    lo = (a - hi.astype(F32)).astype(BF16)
    return hi, lo


def _dot3(a, b, dims=NN_DIMS):
    ah, al = _split_bf16(a)
    bh, bl = _split_bf16(b)
    return _dot(ah, bl, dims) + _dot(al, bh, dims) + _dot(ah, bh, dims)


def _rms_mod(x, nw, scale, shift):
    ms = jnp.mean(x * x, axis=-1, keepdims=True)
    y = x * lax.rsqrt(ms + NORM_EPS) * nw
    return y * (1.0 + scale) + shift


def _store_row_tiles(dst_ref, x):
    n, d = x.shape
    nc = d // LANES
    for c in range(nc):
        dst_ref[pl.ds(c, n, stride=nc), :] = x[:, c * LANES:(c + 1) * LANES]


def _load_row_tiles(src_ref, n, nc):
    return jnp.concatenate([src_ref[pl.ds(c, n, stride=nc), :] for c in range(nc)], axis=1)


def _mod_spec(layer, chunk, d, per_b):
    return pl.BlockSpec((1, 1, 1, d), lambda i: (layer, i // per_b, 0, chunk))


def _layer_spec(layer, shape):
    zeros = (0,) * len(shape)
    return pl.BlockSpec((1,) + tuple(shape), lambda *_: (layer,) + zeros)


def _ada_kernel(c_ref, w_ref, b_ref, o_ref):
    c = c_ref[...]
    c_act = c * jax.nn.sigmoid(c)
    o_ref[0] = _dot3(c_act, w_ref[0]) + b_ref[0]


def _ada_mod(c, ada_w, ada_b):
    depth, d, n = ada_w.shape
    b = c.shape[0]
    tn = n // 4
    return pl.pallas_call(
        _ada_kernel,
        grid=(depth, n // tn),
        in_specs=[
            pl.BlockSpec((b, d), lambda l, j: (0, 0)),
            pl.BlockSpec((1, d, tn), lambda l, j: (l, 0, j)),
            pl.BlockSpec((1, 1, tn), lambda l, j: (l, 0, j)),
        ],
        out_specs=pl.BlockSpec((1, b, tn), lambda l, j: (l, 0, j)),
        out_shape=jax.ShapeDtypeStruct((depth, b, n), F32),
        compiler_params=_params("arbitrary", "arbitrary"),
        name="ada_mod",
    )(c, ada_w, ada_b.reshape(depth, 1, n))


def _inproj_kernel(x_ref, nw_ref, sc_ref, sh_ref, w_ref, ret_ref, moba_ref, lru_ref):
    h = _rms_mod(x_ref[...], nw_ref[0], sc_ref[0, 0], sh_ref[0, 0]).astype(BF16)
    o0 = RET_W * 4
    o1 = o0 + MOBA_W * 3
    ret_ref[...] = _dot(h, w_ref[0, :, 0:o0])
    moba_ref[...] = _dot(h, w_ref[0, :, o0:o1])
    lru_ref[...] = _dot(h, w_ref[0, :, o1:])


def _inproj(layer, x2d, norm_w, mod, w_bf16, seq):
    t, d = x2d.shape
    tm = ROW_TILE
    per_b = seq // tm
    n_ret, n_moba, n_lru = RET_W * 4, MOBA_W * 3, LRU_W * 2
    return pl.pallas_call(
        _inproj_kernel,
        grid=(t // tm,),
        in_specs=[
            pl.BlockSpec((tm, d), lambda i: (i, 0)),
            _layer_spec(layer, (1, d)),
            _mod_spec(layer, 1, d, per_b),
            _mod_spec(layer, 0, d, per_b),
            _layer_spec(layer, w_bf16.shape[1:]),
        ],
        out_specs=[
            pl.BlockSpec((tm, n_ret), lambda i: (i, 0)),
            pl.BlockSpec((tm, n_moba), lambda i: (i, 0)),
            pl.BlockSpec((tm, n_lru), lambda i: (i, 0)),
        ],
        out_shape=[
            jax.ShapeDtypeStruct((t, n_ret), F32),
            jax.ShapeDtypeStruct((t, n_moba), F32),
            jax.ShapeDtypeStruct((t, n_lru), F32),
        ],
        compiler_params=_params("arbitrary"),
        name="inproj",
    )(x2d, norm_w, mod, mod, w_bf16)


def _retention_tables(seq):
    h, dh, c = RET_HEADS, HEAD_DIM, RET_CHUNK
    half = dh // 2
    inv = 1.0 / (ROPE_BASE ** (jnp.arange(half, dtype=F32) / half))
    ang = jnp.arange(seq).astype(F32)[:, None] * inv[None, :]
    cos, sin = jnp.cos(ang), jnp.sin(ang)
    cos_t = jnp.tile(jnp.concatenate([cos, cos], axis=-1), (1, h))
    sin_t = jnp.tile(jnp.concatenate([-sin, sin], axis=-1), (1, h))
    log_g = jnp.log1p(-jnp.exp2(-5.0 - jnp.arange(h, dtype=F32)))
    idx = jnp.arange(c, dtype=F32)
    diff = idx[:, None] - idx[None, :]
    dmat = jnp.where(diff >= 0, jnp.exp(log_g[:, None, None] * jnp.maximum(diff, 0.0)), 0.0)
    zeta = jnp.exp(log_g[:, None] * (c - 1 - idx)[None, :])
    xi = jnp.exp(log_g[:, None] * (idx + 1)[None, :])
    zeta_t = jnp.repeat(zeta.T, dh, axis=1)
    xi_t = jnp.repeat(xi.T, dh, axis=1)
    decay = jnp.broadcast_to(jnp.exp(log_g * c)[:, None, None], (h, dh, dh))
    return cos_t, sin_t, dmat, zeta_t, xi_t, decay


def _retention_kernel(p_ref, cos_ref, sin_ref, dmat_ref, zeta_ref, xi_ref, dec_ref, nw_ref, o_ref):
    seq = p_ref.shape[0]
    c, dh, w = RET_CHUNK, HEAD_DIM, RET_W
    lane = lax.broadcasted_iota(I32, (1, w), 1)
    first_half = (lane % dh) < (dh // 2)

    def rope(x, cos, sin):
        partner = jnp.where(first_half, pltpu.roll(x, w - dh // 2, 1), pltpu.roll(x, dh // 2, 1))
        return x * cos + partner * sin

    def chunk(n, states):
        r0 = pl.multiple_of(n * c, c)
        rows = pl.ds(r0, c)
        cos, sin = cos_ref[rows, :], sin_ref[rows, :]
        q = rope(p_ref[rows, 0:w], cos, sin)
        k = rope(p_ref[rows, w:2 * w], cos, sin) * (dh ** -0.5)
        v = p_ref[rows, 2 * w:3 * w]
        g = p_ref[rows, 3 * w:4 * w]
        kz = k * zeta_ref[...]
        gate = g * jax.nn.sigmoid(g) * nw_ref[0]
        new_states = []
        for hd in range(RET_HEADS):
            cols = slice(hd * dh, (hd + 1) * dh)
            qh = q[:, cols].astype(BF16)
            kh = k[:, cols].astype(BF16)
            vh = v[:, cols].astype(BF16)
            scores = _dot(qh, kh, NT_DIMS) * dmat_ref[hd]
            intra = _dot(scores.astype(BF16), vh)
            cross = _dot(qh, states[hd].astype(BF16)) * xi_ref[:, cols]
            kv = _dot(kz[:, cols].astype(BF16), vh, TN_DIMS)
            y = intra + cross
            mu = jnp.mean(y, axis=-1, keepdims=True)
            yc = y - mu
            var = jnp.mean(yc * yc, axis=-1, keepdims=True)
            o_ref[rows, cols] = yc * lax.rsqrt(var + NORM_EPS) * gate[:, cols]
            new_states.append(states[hd] * dec_ref[hd] + kv)
        return tuple(new_states)

    init = tuple(jnp.zeros((dh, dh), F32) for _ in range(RET_HEADS))
    lax.fori_loop(0, seq // c, chunk, init)


def _retention(layer, ret, norm_w, tables, batch, seq):
    cos_t, sin_t, dmat, zeta_t, xi_t, decay = tables
    w = RET_W
    const2 = lambda b: (0, 0)
    const3 = lambda b: (0, 0, 0)
    return pl.pallas_call(
        _retention_kernel,
        grid=(batch,),
        in_specs=[
            pl.BlockSpec((seq, 4 * w), lambda b: (b, 0)),
            pl.BlockSpec((seq, w), const2),
            pl.BlockSpec((seq, w), const2),
            pl.BlockSpec(dmat.shape, const3),
            pl.BlockSpec(zeta_t.shape, const2),
            pl.BlockSpec(xi_t.shape, const2),
            pl.BlockSpec(decay.shape, const3),
            _layer_spec(layer, (1, w)),
        ],
        out_specs=pl.BlockSpec((seq, w), lambda b: (b, 0)),
        out_shape=jax.ShapeDtypeStruct((batch * seq, w), F32),
        compiler_params=_params("arbitrary"),
        name="retention",
    )(ret, cos_t, sin_t, dmat, zeta_t, xi_t, decay, norm_w)


def _moba_kernel(q_ref, k_ref, v_ref, o_ref):
    seq = q_ref.shape[0]
    bk, dh = MOBA_BLOCK, HEAD_DIM
    nb = seq // bk
    scale = dh ** -0.5
    neg = -jnp.inf
    lane = lax.broadcasted_iota(I32, (1, LANES), 1)
    k_all = k_ref[...]
    k_bf = k_all.astype(BF16)
    v_bf = v_ref[...].astype(BF16)
    kmean = jnp.mean(k_all.reshape(nb, bk, LANES), axis=1)
    colid = lax.broadcasted_iota(I32, (bk, nb), 1)
    causal = (lax.broadcasted_iota(I32, (bk, bk), 0)
              >= lax.broadcasted_iota(I32, (bk, bk), 1))
    heads_per_step = LANES // dh
    outs = [None] * nb
    for p in range(heads_per_step):
        head_lanes = (lane >= p * dh) & (lane < (p + 1) * dh)
        qm = jnp.where(head_lanes, q_ref[...], 0.0)
        gs = _dot3(qm, kmean, NT_DIMS)
        qm_bf = qm.astype(BF16)
        for i in range(nb):
            rows = slice(i * bk, (i + 1) * bk)
            klen = (i + 1) * bk
            s = _dot(qm_bf[rows], k_bf[:klen], NT_DIMS) * scale
            pieces = []
            if i > MOBA_TOPK:
                gsi = gs[rows]
                rank = jnp.zeros((bk, nb), F32)
                for j2 in range(i):
                    cj = gsi[:, j2:j2 + 1]
                    tie = jnp.where(colid > j2, 1.0, 0.0)
                    rank = rank + jnp.where(cj > gsi, 1.0, jnp.where(cj == gsi, tie, 0.0))
                for j in range(i):
                    keep = rank[:, j:j + 1] < float(MOBA_TOPK)
                    pieces.append(jnp.where(keep, s[:, j * bk:(j + 1) * bk], neg))
            else:
                for j in range(i):
                    pieces.append(s[:, j * bk:(j + 1) * bk])
            pieces.append(jnp.where(causal, s[:, i * bk:klen], neg))
            m = pieces[0]
            for pc in pieces[1:]:
                m = jnp.maximum(m, pc)
            m = jnp.max(m, axis=-1, keepdims=True)
            probs = [jnp.exp(pc - m) for pc in pieces]
            tot = probs[0]
            for pr in probs[1:]:
                tot = tot + pr
            denom = jnp.sum(tot, axis=-1, keepdims=True)
            pcat = probs[0] if len(probs) == 1 else jnp.concatenate(probs, axis=1)
            o = _dot(pcat.astype(BF16), v_bf[:klen]) / denom
            outs[i] = o if outs[i] is None else jnp.where(head_lanes, o, outs[i])
    for i in range(nb):
        o_ref[i * bk:(i + 1) * bk, :] = outs[i]


def _moba(moba, batch, seq):
    npair = MOBA_W // LANES
    return pl.pallas_call(
        _moba_kernel,
        grid=(batch, npair),
        in_specs=[
            pl.BlockSpec((seq, LANES), lambda b, p: (b, p)),
            pl.BlockSpec((seq, LANES), lambda b, p: (b, npair + p)),
            pl.BlockSpec((seq, LANES), lambda b, p: (b, 2 * npair + p)),
        ],
        out_specs=pl.BlockSpec((seq, LANES), lambda b, p: (b, p)),
        out_shape=jax.ShapeDtypeStruct((batch * seq, MOBA_W), F32),
        compiler_params=_params("arbitrary", "arbitrary"),
        name="moba",
    )(moba, moba, moba)


def _lru_kernel(xg_ref, cw_ref, cb_ref, wa_ref, ba_ref, wx_ref, bx_ref, lam_ref, o_ref):
    seq = xg_ref.shape[0]
    w = LRU_W
    xb = xg_ref[:, 0:w]
    row = lax.broadcasted_iota(I32, (seq, w), 0)
    u = cb_ref[0] + cw_ref[0, CONV_WIDTH - 1:CONV_WIDTH, :] * xb
    for tap in range(CONV_WIDTH - 1):
        back = CONV_WIDTH - 1 - tap
        u = u + cw_ref[0, tap:tap + 1, :] * jnp.where(row >= back, pltpu.roll(xb, back, 0), 0.0)
    u_bf = u.astype(BF16)
    r = jax.nn.sigmoid(_dot(u_bf, wa_ref[0]) + ba_ref[0])
    ig = jax.nn.sigmoid(_dot(u_bf, wx_ref[0]) + bx_ref[0])
    z = -lam_ref[0]
    softplus = jnp.maximum(z, 0.0) + jnp.log1p(jnp.exp(-jnp.abs(z)))
    log_a = -LRU_C * r * softplus
    a = jnp.exp(log_a)
    b = jnp.sqrt(-jnp.tanh(log_a) * (a * a + 1.0)) * (ig * u)
    d = 1
    while d < seq:
        live = row >= d
        a_prev = jnp.where(live, pltpu.roll(a, d, 0), 1.0)
        b_prev = jnp.where(live, pltpu.roll(b, d, 0), 0.0)
        b = a * b_prev + b
        a = a * a_prev
        d *= 2
    o_ref[...] = jax.nn.gelu(xg_ref[:, w:2 * w]) * b


def _block_diag(wb):
    depth, g, n, _ = wb.shape
    eye = jnp.eye(g, dtype=wb.dtype)
    return (eye[None, :, None, :, None] * wb[:, :, :, None, :]).reshape(depth, g * n, g * n)


def _lru(layer, lru, cw, cb, wa_bd, ba, wx_bd, bx, lam, batch, seq):
    w = LRU_W
    vec = _layer_spec(layer, (1, w))
    mat = _layer_spec(layer, (w, w))
    return pl.pallas_call(
        _lru_kernel,
        grid=(batch,),
        in_specs=[
            pl.BlockSpec((seq, 2 * w), lambda b: (b, 0)),
            _layer_spec(layer, (CONV_WIDTH, w)),
            vec, mat, vec, mat, vec, vec,
        ],
        out_specs=pl.BlockSpec((seq, w), lambda b: (b, 0)),
        out_shape=jax.ShapeDtypeStruct((batch * seq, w), F32),
        compiler_params=_params("arbitrary"),
        name="rg_lru",
    )(lru, cw, cb, wa_bd, ba, wx_bd, bx, lam)


def _outproj_router_kernel(x_ref, yr_ref, ym_ref, yl_ref, wo_ref, g1_ref, nw_ref,
                           sc_ref, sh_ref, rw_ref, rb_ref,
                           x1_ref, h2_ref, ti_ref, tg_ref, rk_ref, cnt_ref, carry_ref):
    i = pl.program_id(0)
    tm = x_ref.shape[0]
    ne = N_EXPERTS
    o0, o1 = RET_W, RET_W + MOBA_W

    @pl.when(i == 0)
    def _():
        carry_ref[...] = jnp.zeros_like(carry_ref)

    mixed = (_dot(yr_ref[...].astype(BF16), wo_ref[0, 0:o0, :])
             + _dot(ym_ref[...].astype(BF16), wo_ref[0, o0:o1, :])
             + _dot(yl_ref[...].astype(BF16), wo_ref[0, o1:, :]))
    x1 = x_ref[...] + g1_ref[0, 0] * mixed
    x1_ref[...] = x1
    h2 = _rms_mod(x1, nw_ref[0], sc_ref[0, 0], sh_ref[0, 0])
    _store_row_tiles(h2_ref, h2)

    logits = _dot3(rw_ref[0], h2, NT_DIMS) + rb_ref[0]
    eid = lax.broadcasted_iota(I32, (ne, tm), 0)
    work = logits
    member = jnp.zeros((ne, tm), F32)
    vals, hots = [], []
    for k in range(TOPK_EXPERTS):
        m = jnp.max(work, axis=0, keepdims=True)
        idx = jnp.min(jnp.where(work == m, eid, ne), axis=0, keepdims=True)
        hot = eid == idx
        ti_ref[k:k + 1, :] = idx
        vals.append(m)
        hots.append(hot)
        member = member + jnp.where(hot, 1.0, 0.0)
        work = jnp.where(hot, -jnp.inf, work)
    exps = [jnp.exp(v - vals[0]) for v in vals]
    tot = exps[0]
    for e in exps[1:]:
        tot = tot + e
    for k in range(TOPK_EXPERTS):
        tg_ref[k:k + 1, :] = exps[k] / tot

    before = (lax.broadcasted_iota(I32, (tm, tm), 0)
              < lax.broadcasted_iota(I32, (tm, tm), 1))
    prefix = _dot(member.astype(BF16), jnp.where(before, 1.0, 0.0).astype(BF16))
    prefix = prefix + carry_ref[:, 0:1]
    for k in range(TOPK_EXPERTS):
        rk = jnp.sum(jnp.where(hots[k], prefix, 0.0), axis=0, keepdims=True)
        rk_ref[k:k + 1, :] = rk.astype(I32)
    carry_ref[...] = carry_ref[...] + jnp.sum(member, axis=1, keepdims=True)
    cnt_ref[...] = carry_ref[...].astype(I32)


def _outproj_router(layer, x2d, y_ret, y_moba, y_lru, wo_bf16, mod, norm_w, rw_t, rb, seq):
    t, d = x2d.shape
    tm = ROW_TILE
    per_b = seq // tm
    ne, kk = N_EXPERTS, TOPK_EXPERTS
    const = lambda i: (0, 0)
    row = lambda n: pl.BlockSpec((tm, n), lambda i: (i, 0))
    lanes = pl.BlockSpec((kk, tm), lambda i: (0, i))
    return pl.pallas_call(
        _outproj_router_kernel,
        grid=(t // tm,),
        in_specs=[
            row(d), row(RET_W), row(MOBA_W), row(LRU_W),
            _layer_spec(layer, wo_bf16.shape[1:]),
            _mod_spec(layer, 2, d, per_b),
            _layer_spec(layer, (1, d)),
            _mod_spec(layer, 4, d, per_b),
            _mod_spec(layer, 3, d, per_b),
            _layer_spec(layer, (ne, d)),
            _layer_spec(layer, (ne, 1)),
        ],
        out_specs=[
            row(d), pl.BlockSpec((tm * (d // LANES), LANES), lambda i: (i, 0)), lanes, lanes, lanes,
            pl.BlockSpec((ne, LANES), const),
        ],
        out_shape=[
            jax.ShapeDtypeStruct((t, d), F32),
            jax.ShapeDtypeStruct((t * (d // LANES), LANES), F32),
            jax.ShapeDtypeStruct((kk, t), I32),
            jax.ShapeDtypeStruct((kk, t), F32),
            jax.ShapeDtypeStruct((kk, t), I32),
            jax.ShapeDtypeStruct((ne, LANES), I32),
        ],
        scratch_shapes=[pltpu.VMEM((ne, LANES), F32)],
        compiler_params=_params("arbitrary"),
        name="outproj_router",
    )(x2d, y_ret, y_moba, y_lru, wo_bf16, mod, norm_w, mod, mod, rw_t, rb)


def _plan_kernel(ti_ref, rk_ref, cnt_ref, pos_ref, blk_ref, meta_ref):
    ne = N_EXPERTS
    tm = ti_ref.shape[1]
    nblk_pad = blk_ref.shape[1]
    shift = EXPERT_ROWS.bit_length() - 1
    counts = cnt_ref[...]
    padded = ((counts + (EXPERT_ROWS - 1)) >> shift) << shift
    row = lax.broadcasted_iota(I32, (ne, LANES), 0)
    pend = padded
    d = 1
    while d < ne:
        pend = pend + jnp.where(row >= d, pltpu.roll(pend, d, 0), 0)
        d *= 2
    pstart = (pend - padded)[:, 0:1]
    eid = lax.broadcasted_iota(I32, (ne, tm), 0)
    for k in range(TOPK_EXPERTS):
        base = jnp.sum(jnp.where(ti_ref[k:k + 1, :] == eid, pstart, 0), axis=0, keepdims=True)
        pos_ref[k:k + 1, :] = base + rk_ref[k:k + 1, :]
    starts = lax.broadcasted_iota(I32, (ne, nblk_pad), 1) * EXPERT_ROWS
    owner = jnp.sum(jnp.where(pend[:, 0:1] <= starts, 1, 0), axis=0, keepdims=True)
    owner = jnp.minimum(owner, ne - 1)
    blk_ref[0:1, :] = owner
    eid_blk = lax.broadcasted_iota(I32, (ne, nblk_pad), 0)
    later = (eid_blk > owner) & (padded[:, 0:1] > 0)
    blk_ref[1:2, :] = jnp.min(jnp.where(later, eid_blk, ne), axis=0, keepdims=True)
    lane = lax.broadcasted_iota(I32, (ne, LANES), 1)
    pend_lanes = jnp.sum(jnp.where(row == lane, pend, 0), axis=0, keepdims=True)
    total = jnp.max(pend, axis=0, keepdims=True)
    meta_ref[0:1, :] = pend_lanes
    meta_ref[1:2, :] = total >> shift
    meta_ref[2:3, :] = jnp.sum(jnp.where(row == lane, padded, 0), axis=0, keepdims=True)
    meta_ref[3:8, :] = jnp.zeros((5, LANES), I32)


def _plan(top_i, rank, counts, n_blocks):
    kk, t = top_i.shape
    tm = PLAN_ROWS
    nblk_pad = -(-n_blocks // LANES) * LANES
    const = lambda i: (0, 0)
    return pl.pallas_call(
        _plan_kernel,
        grid=(t // tm,),
        in_specs=[
            pl.BlockSpec((kk, tm), lambda i: (0, i)),
            pl.BlockSpec((kk, tm), lambda i: (0, i)),
            pl.BlockSpec(counts.shape, const),
        ],
        out_specs=[
            pl.BlockSpec((kk, tm), lambda i: (0, i)),
            pl.BlockSpec((2, nblk_pad), const),
            pl.BlockSpec((8, LANES), const),
        ],
        out_shape=[
            jax.ShapeDtypeStruct((kk, t), I32),
            jax.ShapeDtypeStruct((2, nblk_pad), I32),
            jax.ShapeDtypeStruct((8, LANES), I32),
        ],
        compiler_params=_params("arbitrary"),
        name="plan",
    )(top_i, rank, counts)


def _slab_rows(row, n):
    start = row * ROW_SLABS
    if not isinstance(start, int):
        start = pl.multiple_of(start, ROW_SLABS)
    return pl.ds(start, n * ROW_SLABS)


def _rows_copy(src_ref, src_row, dst_ref, dst_row, sem, n=1):
    return pltpu.make_async_copy(src_ref.at[_slab_rows(src_row, n)], dst_ref.at[_slab_rows(dst_row, n)], sem)


def _dispatch_kernel(meta_ref, pos_ref, h_ref, xr_ref, zeros_ref, hbuf, fill_sem, row_sem, load_sems):
    i = pl.program_id(0)
    last = pl.num_programs(0) - 1
    tm = pos_ref.shape[1]
    blk = EXPERT_ROWS
    n_blocks = xr_ref.shape[0] // (blk * ROW_SLABS)

    def fill(dst_row):
        return _rows_copy(zeros_ref, 0, xr_ref, dst_row, fill_sem, blk)

    def owns_rows(e):
        return meta_ref[2, e] > 0

    @pl.when(i == 0)
    def _():
        zeros_ref[...] = jnp.zeros_like(zeros_ref)
        for e in range(N_EXPERTS):
            @pl.when(owns_rows(e))
            def _():
                fill(meta_ref[0, e] - blk).start()

        def tail_start(b, carry):
            fill(b * blk).start()
            return carry

        def tail_wait(b, carry):
            fill(b * blk).wait()
            return carry

        lax.fori_loop(meta_ref[1, 0], n_blocks, tail_start, 0)
        for e in range(N_EXPERTS):
            @pl.when(owns_rows(e))
            def _():
                fill(meta_ref[0, e] - blk).wait()
        lax.fori_loop(meta_ref[1, 0], n_blocks, tail_wait, 0)

    slot = i % 2

    def load(step, dst_slot):
        return pltpu.make_async_copy(h_ref.at[_slab_rows(step * tm, tm)], hbuf.at[dst_slot],
                                     load_sems.at[dst_slot])

    def issue(g, carry):
        for u in range(MOVE_UNROLL):
            t = g * MOVE_UNROLL + u
            for k in range(TOPK_EXPERTS):
                _rows_copy(hbuf.at[slot], t, xr_ref, pos_ref[k, t], row_sem).start(priority=(u + k) % 2)
        return carry

    def drain(g, carry):
        for u in range(MOVE_UNROLL * TOPK_EXPERTS):
            _rows_copy(hbuf.at[0], 0, xr_ref, 0, row_sem).wait()
        return carry

    @pl.when(i == 0)
    def _():
        load(0, 0).start()

    @pl.when(i > 0)
    def _():
        lax.fori_loop(0, tm // MOVE_UNROLL, drain, 0)

    @pl.when(i < last)
    def _():
        load(i + 1, 1 - slot).start()

    load(i, slot).wait()
    lax.fori_loop(0, tm // MOVE_UNROLL, issue, 0)

    @pl.when(i == last)
    def _():
        lax.fori_loop(0, tm // MOVE_UNROLL, drain, 0)


def _dispatch(meta, pos, h2_tiles, n_rows):
    t = h2_tiles.shape[0] // ROW_SLABS
    tm = MOVE_ROWS
    grid_spec = pltpu.PrefetchScalarGridSpec(
        num_scalar_prefetch=1,
        grid=(t // tm,),
        in_specs=[
            pl.BlockSpec((TOPK_EXPERTS, tm), lambda i, meta: (0, i), memory_space=pltpu.SMEM),
            pl.BlockSpec(memory_space=pl.ANY),
        ],
        out_specs=pl.BlockSpec(memory_space=pl.ANY),
        scratch_shapes=[pltpu.VMEM((EXPERT_ROWS * ROW_SLABS, LANES), F32),
                        pltpu.VMEM((2, tm * ROW_SLABS, LANES), F32),
                        pltpu.SemaphoreType.DMA(()), pltpu.SemaphoreType.DMA(()),
                        pltpu.SemaphoreType.DMA((2,))],
    )
    return pl.pallas_call(
        _dispatch_kernel,
        grid_spec=grid_spec,
        out_shape=jax.ShapeDtypeStruct((n_rows * ROW_SLABS, LANES), F32),
        compiler_params=_params("arbitrary"),
        name="dispatch",
    )(meta, pos, h2_tiles)


def _expert_kernel(be_ref, meta_ref, x_ref, wgu_ref, bgu_ref, wdn_ref, bdn_ref, y_ref,
                   wgu_stage, wdn_stage, wgu_bf, wdn_bf, sems, *, layer):
    i = pl.program_id(0)
    ff = wdn_bf.shape[0]
    n_used = meta_ref[1, 0]

    def fetch(e):
        return (pltpu.make_async_copy(wgu_ref.at[layer, e], wgu_stage, sems.at[0]),
                pltpu.make_async_copy(wdn_ref.at[layer, e], wdn_stage, sems.at[1]))

    @pl.when(i >= n_used)
    def _():
        y_ref[...] = jnp.zeros_like(y_ref)

    @pl.when(i < n_used)
    def _():
        cur = be_ref[0, i]
        prev = be_ref[0, jnp.maximum(i - 1, 0)]

        @pl.when(i == 0)
        def _():
            for cp in fetch(cur):
                cp.start()

        @pl.when((i == 0) | (cur != prev))
        def _():
            for cp in fetch(cur):
                cp.wait()
            wgu_bf[...] = wgu_stage[...].astype(BF16)
            wdn_bf[...] = wdn_stage[...].astype(BF16)
            nxt = be_ref[1, i]

            @pl.when(nxt < N_EXPERTS)
            def _():
                for cp in fetch(nxt):
                    cp.start()

        x = _load_row_tiles(x_ref, EXPERT_ROWS, ROW_SLABS).astype(BF16)
        gu = _dot(x, wgu_bf[...]) + bgu_ref[0, 0]
        gate = jnp.minimum(gu[:, :ff], SWIGLU_LIMIT)
        up = jnp.clip(gu[:, ff:], -SWIGLU_LIMIT, SWIGLU_LIMIT)
        act = (up + 1.0) * gate * jax.nn.sigmoid(SWIGLU_ALPHA * gate)
        _store_row_tiles(y_ref, _dot(act.astype(BF16), wdn_bf[...]) + bdn_ref[0, 0])


def _experts(layer, blk_e, meta, xr, w_gu, b_gu, w_dn, b_dn):
    n_rows = xr.shape[0] // ROW_SLABS
    d = ROW_SLABS * LANES
    ff2 = w_gu.shape[-1]
    ff = ff2 // 2
    tm = EXPERT_ROWS
    nblk = n_rows // tm

    def last_used(i, meta):
        return jnp.minimum(i, meta[1, 0] - 1)

    xmap = lambda i, be, meta: (last_used(i, meta), 0)
    wmap = lambda i, be, meta: (layer, be[0, last_used(i, meta)], 0, 0)
    grid_spec = pltpu.PrefetchScalarGridSpec(
        num_scalar_prefetch=2,
        grid=(nblk,),
        in_specs=[
            pl.BlockSpec((tm * ROW_SLABS, LANES), xmap),
            pl.BlockSpec(memory_space=pl.ANY),
            pl.BlockSpec((1, 1, 1, ff2), wmap),
            pl.BlockSpec(memory_space=pl.ANY),
            pl.BlockSpec((1, 1, 1, d), wmap),
        ],
        out_specs=pl.BlockSpec((tm * ROW_SLABS, LANES), lambda i, be, meta: (i, 0)),
        scratch_shapes=[pltpu.VMEM((d, ff2), F32), pltpu.VMEM((ff, d), F32),
                        pltpu.VMEM((d, ff2), BF16), pltpu.VMEM((ff, d), BF16),
                        pltpu.SemaphoreType.DMA((2,))],
    )
    return pl.pallas_call(
        functools.partial(_expert_kernel, layer=layer),
        grid_spec=grid_spec,
        out_shape=jax.ShapeDtypeStruct((n_rows * ROW_SLABS, LANES), F32),
        compiler_params=_params("arbitrary"),
        name="experts",
    )(blk_e, meta, xr, w_gu, b_gu, w_dn, b_dn)


def _combine_kernel(pos_ref, nxt_ref, x1_ref, g_ref, g2_ref, fw_ref, yr_ref, o_ref, buf, sems, *, final_norm):
    i = pl.program_id(0)
    n = pl.num_programs(0)
    tm = x1_ref.shape[0]
    slot = i % 2

    def gather(table_ref, dst_slot):
        def issue(g, carry):
            for u in range(MOVE_UNROLL):
                t = g * MOVE_UNROLL + u
                for k in range(TOPK_EXPERTS):
                    _rows_copy(yr_ref, table_ref[k, t], buf.at[dst_slot, k], t,
                               sems.at[dst_slot]).start(priority=(u + k) % 2)
            return carry
        lax.fori_loop(0, tm // MOVE_UNROLL, issue, 0)

    def drain(g, carry):
        for u in range(MOVE_UNROLL * TOPK_EXPERTS):
            _rows_copy(yr_ref, 0, buf.at[slot, 0], 0, sems.at[slot]).wait()
        return carry

    @pl.when(i == 0)
    def _():
        gather(pos_ref, 0)

    @pl.when(i + 1 < n)
    def _():
        gather(nxt_ref, 1 - slot)

    lax.fori_loop(0, tm // MOVE_UNROLL, drain, 0)
    acc = g_ref[:, 0:1] * _load_row_tiles(buf.at[slot, 0], tm, ROW_SLABS)
    for k in range(1, TOPK_EXPERTS):
        acc = acc + g_ref[:, k:k + 1] * _load_row_tiles(buf.at[slot, k], tm, ROW_SLABS)
    x2 = x1_ref[...] + g2_ref[0, 0] * acc
    if final_norm:
        ms = jnp.mean(x2 * x2, axis=-1, keepdims=True)
        x2 = x2 * lax.rsqrt(ms + NORM_EPS) * fw_ref[...]
    o_ref[...] = x2


def _combine(layer, pos, x1, gates, mod, final_w, yr, seq, final_norm):
    t, d = x1.shape
    tm = MOVE_ROWS
    per_b = seq // tm
    kk = TOPK_EXPERTS
    n = t // tm
    return pl.pallas_call(
        functools.partial(_combine_kernel, final_norm=final_norm),
        grid=(n,),
        in_specs=[
            pl.BlockSpec((kk, tm), lambda i: (0, i), memory_space=pltpu.SMEM),
            pl.BlockSpec((kk, tm), lambda i: (0, jnp.minimum(i + 1, n - 1)), memory_space=pltpu.SMEM),
            pl.BlockSpec((tm, d), lambda i: (i, 0)),
            pl.BlockSpec((tm, kk), lambda i: (i, 0)),
            _mod_spec(layer, 5, d, per_b),
            pl.BlockSpec((1, d), lambda i: (0, 0)),
            pl.BlockSpec(memory_space=pl.ANY),
        ],
        out_specs=pl.BlockSpec((tm, d), lambda i: (i, 0)),
        out_shape=jax.ShapeDtypeStruct((t, d), F32),
        scratch_shapes=[pltpu.VMEM((2, kk, tm * ROW_SLABS, LANES), F32), pltpu.SemaphoreType.DMA((2,))],
        compiler_params=_params("arbitrary"),
        name="combine",
    )(pos, pos, x1, gates, mod, final_w, yr)


def kernel(x, c, ada_w, ada_b, norm_mix_w, w_in, ret_norm_w, lru_conv_w, lru_conv_b, lru_gate_a_w,
           lru_gate_a_b, lru_gate_x_w, lru_gate_x_b, lru_lambda, w_out, norm_ffn_w, router_w, router_b,
           moe_w_gu, moe_b_gu, moe_w_down, moe_b_down, final_norm_w):
    batch, seq, d = x.shape
    assert d == ROW_SLABS * LANES, "MoE row buffers store one (8, 128) tile per token row"
    depth = ada_w.shape[0]
    t = batch * seq
    ne = N_EXPERTS
    n_blocks = (t * TOPK_EXPERTS) // EXPERT_ROWS + ne
    tables = _retention_tables(seq)
    mod = _ada_mod(c, ada_w, ada_b).reshape(depth, batch, 1, N_MOD * d)
    w_in_bf = w_in.astype(BF16)
    w_out_bf = w_out.astype(BF16)
    wa_bd = _block_diag(lru_gate_a_w).astype(BF16)
    wx_bd = _block_diag(lru_gate_x_w).astype(BF16)
    rw_t = jnp.swapaxes(router_w, 1, 2)
    vec = lambda p: p.reshape(depth, 1, p.shape[-1])
    b_gu = moe_b_gu.reshape(depth, ne, 1, -1)
    b_dn = moe_b_down.reshape(depth, ne, 1, d)
    xf = x.reshape(t, d)
    for l in range(depth):
        ret, moba, lru = _inproj(l, xf, vec(norm_mix_w), mod, w_in_bf, seq)
        y_ret = _retention(l, ret, vec(ret_norm_w), tables, batch, seq)
        y_moba = _moba(moba, batch, seq)
        y_lru = _lru(l, lru, lru_conv_w, vec(lru_conv_b), wa_bd, vec(lru_gate_a_b), wx_bd,
                     vec(lru_gate_x_b), vec(lru_lambda), batch, seq)
        x1, h2, top_i, gates, rank, counts = _outproj_router(
            l, xf, y_ret, y_moba, y_lru, w_out_bf, mod, vec(norm_ffn_w), rw_t,
            router_b.reshape(depth, ne, 1), seq)
        pos, blk_e, meta = _plan(top_i, rank, counts, n_blocks)
        xr = _dispatch(meta, pos, h2, n_blocks * EXPERT_ROWS)
        yr = _experts(l, blk_e, meta, xr, moe_w_gu, b_gu, moe_w_down, b_dn)
        xf = _combine(l, pos, x1, gates.T, mod, final_norm_w.reshape(1, d), yr, seq,
                      final_norm=(l == depth - 1))
    return xf.reshape(batch, seq, d)
```

```python
import functools

import jax
import jax.numpy as jnp
from jax import lax
from jax.experimental import pallas as pl
from jax.experimental.pallas import tpu as pltpu

F32 = jnp.float32
BF16 = jnp.bfloat16
I32 = jnp.int32

HEAD_DIM = 64
RET_HEADS = 4
RET_W = RET_HEADS * HEAD_DIM
RET_CHUNK = 128
ROPE_BASE = 10000.0
MOBA_HEADS = 6
MOBA_W = MOBA_HEADS * HEAD_DIM
MOBA_BLOCK = 256
MOBA_TOPK = 3
LRU_BLOCKS = 6
LRU_BLOCK_W = 64
LRU_W = LRU_BLOCKS * LRU_BLOCK_W
CONV_WIDTH = 4
LRU_C = 8.0
N_EXPERTS = 32
TOPK_EXPERTS = 4
SWIGLU_LIMIT = 7.0
SWIGLU_ALPHA = 1.702
NORM_EPS = 1e-6
N_MOD = 6

LANES = 128
ROW_SLABS = 8
VMEM_LIMIT = 56 * 1024 * 1024
ROW_TILE = 512
EXPERT_ROWS = 512
MOVE_ROWS = 256
MOVE_UNROLL = 8
PLAN_ROWS = 2048
RET_SEQS = 2
LRU_GROUP = 8

NT_DIMS = (((1,), (1,)), ((), ()))
NN_DIMS = (((1,), (0,)), ((), ()))
TN_DIMS = (((0,), (0,)), ((), ()))


def _params(*semantics):
    return pltpu.CompilerParams(dimension_semantics=semantics, vmem_limit_bytes=VMEM_LIMIT)


def _dot(a, b, dims=NN_DIMS):
    return lax.dot_general(a, b, dims, preferred_element_type=F32)


def _split_bf16(a):
    hi = a.astype(BF16)
    lo = (a - hi.astype(F32)).astype(BF16)
    return hi, lo


def _dot3(a, b, dims=NN_DIMS):
    ah, al = _split_bf16(a)
    bh, bl = _split_bf16(b)
    return _dot(ah, bl, dims) + _dot(al, bh, dims) + _dot(ah, bh, dims)


def _rms_mod(x, nw, scale, shift):
    ms = jnp.mean(x * x, axis=-1, keepdims=True)
    y = x * lax.rsqrt(ms + NORM_EPS) * nw
    return y * (1.0 + scale) + shift


def _store_row_tiles(dst_ref, x):
    n, d = x.shape
    nc = d // LANES
    for c in range(nc):
        dst_ref[pl.ds(c, n, stride=nc), :] = x[:, c * LANES:(c + 1) * LANES]


def _load_row_tiles(src_ref, n, nc):
    return jnp.concatenate([src_ref[pl.ds(c, n, stride=nc), :] for c in range(nc)], axis=1)


def _mod_spec(layer, chunk, d, per_b):
    return pl.BlockSpec((1, 1, 1, d), lambda i: (layer, i // per_b, 0, chunk))


def _layer_spec(layer, shape):
    zeros = (0,) * len(shape)
    return pl.BlockSpec((1,) + tuple(shape), lambda *_: (layer,) + zeros)


def _ada_kernel(c_ref, w_ref, b_ref, o_ref):
    c = c_ref[...]
    c_act = c * jax.nn.sigmoid(c)
    o_ref[0] = _dot3(c_act, w_ref[0]) + b_ref[0]


def _ada_mod(c, ada_w, ada_b):
    depth, d, n = ada_w.shape
    b = c.shape[0]
    tn = n // 4
    return pl.pallas_call(
        _ada_kernel,
        grid=(depth, n // tn),
        in_specs=[
            pl.BlockSpec((b, d), lambda l, j: (0, 0)),
            pl.BlockSpec((1, d, tn), lambda l, j: (l, 0, j)),
            pl.BlockSpec((1, 1, tn), lambda l, j: (l, 0, j)),
        ],
        out_specs=pl.BlockSpec((1, b, tn), lambda l, j: (l, 0, j)),
        out_shape=jax.ShapeDtypeStruct((depth, b, n), F32),
        compiler_params=_params("arbitrary", "arbitrary"),
        name="ada_mod",
    )(c, ada_w, ada_b.reshape(depth, 1, n))


def _inproj_kernel(x_ref, nw_ref, sc_ref, sh_ref, w_ref, ret_ref, moba_ref, lru_ref):
    h = _rms_mod(x_ref[...], nw_ref[0], sc_ref[0, 0], sh_ref[0, 0]).astype(BF16)
    o0 = RET_W * 4
    o1 = o0 + MOBA_W * 3
    ret_ref[...] = _dot(h, w_ref[0, :, 0:o0])
    moba_ref[...] = _dot(h, w_ref[0, :, o0:o1])
    lru_ref[...] = _dot(h, w_ref[0, :, o1:])


def _inproj(layer, x2d, norm_w, mod, w_bf16, seq):
    t, d = x2d.shape
    tm = ROW_TILE
    per_b = seq // tm
    n_ret, n_moba, n_lru = RET_W * 4, MOBA_W * 3, LRU_W * 2
    return pl.pallas_call(
        _inproj_kernel,
        grid=(t // tm,),
        in_specs=[
            pl.BlockSpec((tm, d), lambda i: (i, 0)),
            _layer_spec(layer, (1, d)),
            _mod_spec(layer, 1, d, per_b),
            _mod_spec(layer, 0, d, per_b),
            _layer_spec(layer, w_bf16.shape[1:]),
        ],
        out_specs=[
            pl.BlockSpec((tm, n_ret), lambda i: (i, 0)),
            pl.BlockSpec((tm, n_moba), lambda i: (i, 0)),
            pl.BlockSpec((tm, n_lru), lambda i: (i, 0)),
        ],
        out_shape=[
            jax.ShapeDtypeStruct((t, n_ret), F32),
            jax.ShapeDtypeStruct((t, n_moba), F32),
            jax.ShapeDtypeStruct((t, n_lru), F32),
        ],
        compiler_params=_params("arbitrary"),
        name="inproj",
    )(x2d, norm_w, mod, mod, w_bf16)


def _retention_tables(seq):
    h, dh, c = RET_HEADS, HEAD_DIM, RET_CHUNK
    half = dh // 2
    inv = 1.0 / (ROPE_BASE ** (jnp.arange(half, dtype=F32) / half))
    ang = jnp.arange(seq).astype(F32)[:, None] * inv[None, :]
    cos, sin = jnp.cos(ang), jnp.sin(ang)
    cos_t = jnp.tile(jnp.concatenate([cos, cos], axis=-1), (1, h))
    sin_t = jnp.tile(jnp.concatenate([-sin, sin], axis=-1), (1, h))
    log_g = jnp.log1p(-jnp.exp2(-5.0 - jnp.arange(h, dtype=F32)))
    idx = jnp.arange(c, dtype=F32)
    diff = idx[:, None] - idx[None, :]
    dmat = jnp.where(diff >= 0, jnp.exp(log_g[:, None, None] * jnp.maximum(diff, 0.0)), 0.0)
    zeta = jnp.exp(log_g[:, None] * (c - 1 - idx)[None, :])
    xi = jnp.exp(log_g[:, None] * (idx + 1)[None, :])
    zeta_t = jnp.repeat(zeta.T, dh, axis=1)
    xi_t = jnp.repeat(xi.T, dh, axis=1)
    decay = jnp.broadcast_to(jnp.exp(log_g * c)[:, None, None], (h, dh, dh))
    return cos_t, sin_t, dmat, zeta_t, xi_t, decay


def _retention_kernel(p_ref, cos_ref, sin_ref, dmat_ref, zeta_ref, xi_ref, dec_ref, nw_ref, o_ref):
    seq = cos_ref.shape[0]
    nseq = p_ref.shape[0] // seq
    c, dh, w = RET_CHUNK, HEAD_DIM, RET_W
    lane = lax.broadcasted_iota(I32, (1, w), 1)
    first_half = (lane % dh) < (dh // 2)

    def rope(x, cos, sin):
        partner = jnp.where(first_half, pltpu.roll(x, w - dh // 2, 1), pltpu.roll(x, dh // 2, 1))
        return x * cos + partner * sin

    def chunk(n, states):
        r0 = pl.multiple_of(n * c, c)
        cos, sin = cos_ref[pl.ds(r0, c), :], sin_ref[pl.ds(r0, c), :]
        new_states = []
        for b in range(nseq):
            rows = pl.ds(pl.multiple_of(b * seq + r0, c), c)
            q = rope(p_ref[rows, 0:w], cos, sin)
            k = rope(p_ref[rows, w:2 * w], cos, sin) * (dh ** -0.5)
            v = p_ref[rows, 2 * w:3 * w]
            g = p_ref[rows, 3 * w:4 * w]
            kz = k * zeta_ref[...]
            gate = g * jax.nn.sigmoid(g) * nw_ref[0]
            for hd in range(RET_HEADS):
                state = states[b * RET_HEADS + hd]
                cols = slice(hd * dh, (hd + 1) * dh)
                qh = q[:, cols].astype(BF16)
                kh = k[:, cols].astype(BF16)
                vh = v[:, cols].astype(BF16)
                scores = _dot(qh, kh, NT_DIMS) * dmat_ref[hd]
                intra = _dot(scores.astype(BF16), vh)
                cross = _dot(qh, state.astype(BF16)) * xi_ref[:, cols]
                kv = _dot(kz[:, cols].astype(BF16), vh, TN_DIMS)
                y = intra + cross
                mu = jnp.mean(y, axis=-1, keepdims=True)
                yc = y - mu
                var = jnp.mean(yc * yc, axis=-1, keepdims=True)
                o_ref[rows, cols] = yc * lax.rsqrt(var + NORM_EPS) * gate[:, cols]
                new_states.append(state * dec_ref[hd] + kv)
        return tuple(new_states)

    init = tuple(jnp.zeros((dh, dh), F32) for _ in range(nseq * RET_HEADS))
    lax.fori_loop(0, seq // c, chunk, init)


def _retention(layer, ret, norm_w, tables, batch, seq):
    cos_t, sin_t, dmat, zeta_t, xi_t, decay = tables
    w = RET_W
    const2 = lambda b: (0, 0)
    const3 = lambda b: (0, 0, 0)
    nseq = RET_SEQS if batch % RET_SEQS == 0 else 1
    return pl.pallas_call(
        _retention_kernel,
        grid=(batch // nseq,),
        in_specs=[
            pl.BlockSpec((nseq * seq, 4 * w), lambda b: (b, 0)),
            pl.BlockSpec((seq, w), const2),
            pl.BlockSpec((seq, w), const2),
            pl.BlockSpec(dmat.shape, const3),
            pl.BlockSpec(zeta_t.shape, const2),
            pl.BlockSpec(xi_t.shape, const2),
            pl.BlockSpec(decay.shape, const3),
            _layer_spec(layer, (1, w)),
        ],
        out_specs=pl.BlockSpec((nseq * seq, w), lambda b: (b, 0)),
        out_shape=jax.ShapeDtypeStruct((batch * seq, w), F32),
        compiler_params=_params("arbitrary"),
        name="retention",
    )(ret, cos_t, sin_t, dmat, zeta_t, xi_t, decay, norm_w)


def _moba_kernel(q_ref, k_ref, v_ref, o_ref):
    seq = q_ref.shape[0]
    bk, dh = MOBA_BLOCK, HEAD_DIM
    nb = seq // bk
    scale = dh ** -0.5
    neg = -jnp.inf
    lane = lax.broadcasted_iota(I32, (1, LANES), 1)
    k_all = k_ref[...]
    k_bf = k_all.astype(BF16)
    v_bf = v_ref[...].astype(BF16)
    kmean = jnp.mean(k_all.reshape(nb, bk, LANES), axis=1)
    colid = lax.broadcasted_iota(I32, (bk, nb), 1)
    causal = (lax.broadcasted_iota(I32, (bk, bk), 0)
              >= lax.broadcasted_iota(I32, (bk, bk), 1))
    heads_per_step = LANES // dh
    outs = [None] * nb
    for p in range(heads_per_step):
        head_lanes = (lane >= p * dh) & (lane < (p + 1) * dh)
        qm = jnp.where(head_lanes, q_ref[...], 0.0)
        gs = _dot3(qm, kmean, NT_DIMS)
        qm_bf = (qm * scale).astype(BF16)
        for i in range(nb):
            rows = slice(i * bk, (i + 1) * bk)
            klen = (i + 1) * bk
            s = _dot(qm_bf[rows], k_bf[:klen], NT_DIMS)
            pieces = []
            if i > MOBA_TOPK:
                gsi = gs[rows]
                rank = jnp.zeros((bk, nb), F32)
                for j2 in range(i):
                    cj = gsi[:, j2:j2 + 1]
                    tie = jnp.where(colid > j2, 1.0, 0.0)
                    rank = rank + jnp.where(cj > gsi, 1.0, jnp.where(cj == gsi, tie, 0.0))
                for j in range(i):
                    keep = rank[:, j:j + 1] < float(MOBA_TOPK)
                    pieces.append(jnp.where(keep, s[:, j * bk:(j + 1) * bk], neg))
            else:
                for j in range(i):
                    pieces.append(s[:, j * bk:(j + 1) * bk])
            pieces.append(jnp.where(causal, s[:, i * bk:klen], neg))
            m = pieces[0]
            for pc in pieces[1:]:
                m = jnp.maximum(m, pc)
            m = jnp.max(m, axis=-1, keepdims=True)
            probs = [jnp.exp(pc - m) for pc in pieces]
            tot = probs[0]
            for pr in probs[1:]:
                tot = tot + pr
            denom = jnp.sum(tot, axis=-1, keepdims=True)
            pcat = probs[0] if len(probs) == 1 else jnp.concatenate(probs, axis=1)
            o = _dot(pcat.astype(BF16), v_bf[:klen]) / denom
            outs[i] = o if outs[i] is None else jnp.where(head_lanes, o, outs[i])
    for i in range(nb):
        o_ref[i * bk:(i + 1) * bk, :] = outs[i]


def _moba(moba, batch, seq):
    npair = MOBA_W // LANES
    return pl.pallas_call(
        _moba_kernel,
        grid=(batch, npair),
        in_specs=[
            pl.BlockSpec((seq, LANES), lambda b, p: (b, p)),
            pl.BlockSpec((seq, LANES), lambda b, p: (b, npair + p)),
            pl.BlockSpec((seq, LANES), lambda b, p: (b, 2 * npair + p)),
        ],
        out_specs=pl.BlockSpec((seq, LANES), lambda b, p: (b, p)),
        out_shape=jax.ShapeDtypeStruct((batch * seq, MOBA_W), F32),
        compiler_params=_params("arbitrary", "arbitrary"),
        name="moba",
    )(moba, moba, moba)


def _lru_kernel(xg_ref, cw_ref, cb_ref, wa_ref, ba_ref, wx_ref, bx_ref, lam_ref, o_ref, a_scr, h_scr):
    seq = xg_ref.shape[0]
    w = LRU_W
    xb = xg_ref[:, 0:w]
    row = lax.broadcasted_iota(I32, (seq, w), 0)
    u = cb_ref[0] + cw_ref[0, CONV_WIDTH - 1:CONV_WIDTH, :] * xb
    for tap in range(CONV_WIDTH - 1):
        back = CONV_WIDTH - 1 - tap
        u = u + cw_ref[0, tap:tap + 1, :] * jnp.where(row >= back, pltpu.roll(xb, back, 0), 0.0)
    u_bf = u.astype(BF16)
    r = jax.nn.sigmoid(_dot(u_bf, wa_ref[0]) + ba_ref[0])
    ig = jax.nn.sigmoid(_dot(u_bf, wx_ref[0]) + bx_ref[0])
    z = -lam_ref[0]
    softplus = jnp.maximum(z, 0.0) + jnp.log1p(jnp.exp(-jnp.abs(z)))
    log_a = -LRU_C * r * softplus
    a = jnp.exp(log_a)
    b = jnp.sqrt(-jnp.tanh(log_a) * (a * a + 1.0)) * (ig * u)
    sub = row & (LRU_GROUP - 1)
    d = 1
    while d < LRU_GROUP:
        live = sub >= d
        a_prev = jnp.where(live, pltpu.roll(a, d, 0), 1.0)
        b_prev = jnp.where(live, pltpu.roll(b, d, 0), 0.0)
        b = a * b_prev + b
        a = a * a_prev
        d *= 2
    a_scr[...] = a
    h_scr[...] = b

    def group(gi, carry):
        rows = pl.ds(pl.multiple_of(gi * LRU_GROUP, LRU_GROUP), LRU_GROUP)
        h = a_scr[rows, :] * carry + h_scr[rows, :]
        h_scr[rows, :] = h
        return h[LRU_GROUP - 1:LRU_GROUP, :]

    lax.fori_loop(0, seq // LRU_GROUP, group, jnp.zeros((1, w), F32), unroll=8)
    o_ref[...] = jax.nn.gelu(xg_ref[:, w:2 * w]) * h_scr[...]


def _block_diag(wb):
    depth, g, n, _ = wb.shape
    eye = jnp.eye(g, dtype=wb.dtype)
    return (eye[None, :, None, :, None] * wb[:, :, :, None, :]).reshape(depth, g * n, g * n)


def _lru(layer, lru, cw, cb, wa_bd, ba, wx_bd, bx, lam, batch, seq):
    w = LRU_W
    vec = _layer_spec(layer, (1, w))
    mat = _layer_spec(layer, (w, w))
    return pl.pallas_call(
        _lru_kernel,
        grid=(batch,),
        in_specs=[
            pl.BlockSpec((seq, 2 * w), lambda b: (b, 0)),
            _layer_spec(layer, (CONV_WIDTH, w)),
            vec, mat, vec, mat, vec, vec,
        ],
        out_specs=pl.BlockSpec((seq, w), lambda b: (b, 0)),
        out_shape=jax.ShapeDtypeStruct((batch * seq, w), F32),
        scratch_shapes=[pltpu.VMEM((seq, w), F32), pltpu.VMEM((seq, w), F32)],
        compiler_params=_params("arbitrary"),
        name="rg_lru",
    )(lru, cw, cb, wa_bd, ba, wx_bd, bx, lam)


def _outproj_router_kernel(x_ref, yr_ref, ym_ref, yl_ref, wo_ref, g1_ref, nw_ref,
                           sc_ref, sh_ref, rw_ref, rb_ref,
                           x1_ref, h2_ref, ti_ref, tg_ref, rk_ref, cnt_ref, carry_ref):
    i = pl.program_id(0)
    tm = x_ref.shape[0]
    ne = N_EXPERTS
    o0, o1 = RET_W, RET_W + MOBA_W

    @pl.when(i == 0)
    def _():
        carry_ref[...] = jnp.zeros_like(carry_ref)

    mixed = (_dot(yr_ref[...].astype(BF16), wo_ref[0, 0:o0, :])
             + _dot(ym_ref[...].astype(BF16), wo_ref[0, o0:o1, :])
             + _dot(yl_ref[...].astype(BF16), wo_ref[0, o1:, :]))
    x1 = x_ref[...] + g1_ref[0, 0] * mixed
    x1_ref[...] = x1
    h2 = _rms_mod(x1, nw_ref[0], sc_ref[0, 0], sh_ref[0, 0])
    _store_row_tiles(h2_ref, h2)

    logits = _dot3(rw_ref[0], h2, NT_DIMS) + rb_ref[0]
    eid = lax.broadcasted_iota(I32, (ne, tm), 0)
    work = logits
    member = jnp.zeros((ne, tm), F32)
    vals, hots = [], []
    for k in range(TOPK_EXPERTS):
        m = jnp.max(work, axis=0, keepdims=True)
        idx = jnp.min(jnp.where(work == m, eid, ne), axis=0, keepdims=True)
        hot = eid == idx
        ti_ref[k:k + 1, :] = idx
        vals.append(m)
        hots.append(hot)
        member = member + jnp.where(hot, 1.0, 0.0)
        work = jnp.where(hot, -jnp.inf, work)
    exps = [jnp.exp(v - vals[0]) for v in vals]
    tot = exps[0]
    for e in exps[1:]:
        tot = tot + e
    for k in range(TOPK_EXPERTS):
        tg_ref[k:k + 1, :] = exps[k] / tot

    before = (lax.broadcasted_iota(I32, (tm, tm), 0)
              < lax.broadcasted_iota(I32, (tm, tm), 1))
    prefix = _dot(member.astype(BF16), jnp.where(before, 1.0, 0.0).astype(BF16))
    prefix = prefix + carry_ref[:, 0:1]
    for k in range(TOPK_EXPERTS):
        rk = jnp.sum(jnp.where(hots[k], prefix, 0.0), axis=0, keepdims=True)
        rk_ref[k:k + 1, :] = rk.astype(I32)
    carry_ref[...] = carry_ref[...] + jnp.sum(member, axis=1, keepdims=True)
    cnt_ref[...] = carry_ref[...].astype(I32)


def _outproj_router(layer, x2d, y_ret, y_moba, y_lru, wo_bf16, mod, norm_w, rw_t, rb, seq):
    t, d = x2d.shape
    tm = ROW_TILE
    per_b = seq // tm
    ne, kk = N_EXPERTS, TOPK_EXPERTS
    const = lambda i: (0, 0)
    row = lambda n: pl.BlockSpec((tm, n), lambda i: (i, 0))
    lanes = pl.BlockSpec((kk, tm), lambda i: (0, i))
    return pl.pallas_call(
        _outproj_router_kernel,
        grid=(t // tm,),
        in_specs=[
            row(d), row(RET_W), row(MOBA_W), row(LRU_W),
            _layer_spec(layer, wo_bf16.shape[1:]),
            _mod_spec(layer, 2, d, per_b),
            _layer_spec(layer, (1, d)),
            _mod_spec(layer, 4, d, per_b),
            _mod_spec(layer, 3, d, per_b),
            _layer_spec(layer, (ne, d)),
            _layer_spec(layer, (ne, 1)),
        ],
        out_specs=[
            row(d), pl.BlockSpec((tm * (d // LANES), LANES), lambda i: (i, 0)), lanes, lanes, lanes,
            pl.BlockSpec((ne, LANES), const),
        ],
        out_shape=[
            jax.ShapeDtypeStruct((t, d), F32),
            jax.ShapeDtypeStruct((t * (d // LANES), LANES), F32),
            jax.ShapeDtypeStruct((kk, t), I32),
            jax.ShapeDtypeStruct((kk, t), F32),
            jax.ShapeDtypeStruct((kk, t), I32),
            jax.ShapeDtypeStruct((ne, LANES), I32),
        ],
        scratch_shapes=[pltpu.VMEM((ne, LANES), F32)],
        compiler_params=_params("arbitrary"),
        name="outproj_router",
    )(x2d, y_ret, y_moba, y_lru, wo_bf16, mod, norm_w, mod, mod, rw_t, rb)


def _plan_kernel(ti_ref, rk_ref, cnt_ref, pos_ref, blk_ref, meta_ref):
    ne = N_EXPERTS
    tm = ti_ref.shape[1]
    nblk_pad = blk_ref.shape[1]
    shift = EXPERT_ROWS.bit_length() - 1
    counts = cnt_ref[...]
    padded = ((counts + (EXPERT_ROWS - 1)) >> shift) << shift
    row = lax.broadcasted_iota(I32, (ne, LANES), 0)
    pend = padded
    d = 1
    while d < ne:
        pend = pend + jnp.where(row >= d, pltpu.roll(pend, d, 0), 0)
        d *= 2
    pstart = (pend - padded)[:, 0:1]
    eid = lax.broadcasted_iota(I32, (ne, tm), 0)
    for k in range(TOPK_EXPERTS):
        base = jnp.sum(jnp.where(ti_ref[k:k + 1, :] == eid, pstart, 0), axis=0, keepdims=True)
        pos_ref[k:k + 1, :] = base + rk_ref[k:k + 1, :]
    starts = lax.broadcasted_iota(I32, (ne, nblk_pad), 1) * EXPERT_ROWS
    owner = jnp.sum(jnp.where(pend[:, 0:1] <= starts, 1, 0), axis=0, keepdims=True)
    owner = jnp.minimum(owner, ne - 1)
    blk_ref[0:1, :] = owner
    eid_blk = lax.broadcasted_iota(I32, (ne, nblk_pad), 0)
    later = (eid_blk > owner) & (padded[:, 0:1] > 0)
    blk_ref[1:2, :] = jnp.min(jnp.where(later, eid_blk, ne), axis=0, keepdims=True)
    lane = lax.broadcasted_iota(I32, (ne, LANES), 1)
    pend_lanes = jnp.sum(jnp.where(row == lane, pend, 0), axis=0, keepdims=True)
    total = jnp.max(pend, axis=0, keepdims=True)
    meta_ref[0:1, :] = pend_lanes
    meta_ref[1:2, :] = total >> shift
    meta_ref[2:3, :] = jnp.sum(jnp.where(row == lane, padded, 0), axis=0, keepdims=True)
    meta_ref[3:8, :] = jnp.zeros((5, LANES), I32)


def _plan(top_i, rank, counts, n_blocks):
    kk, t = top_i.shape
    tm = PLAN_ROWS
    nblk_pad = -(-n_blocks // LANES) * LANES
    const = lambda i: (0, 0)
    return pl.pallas_call(
        _plan_kernel,
        grid=(t // tm,),
        in_specs=[
            pl.BlockSpec((kk, tm), lambda i: (0, i)),
            pl.BlockSpec((kk, tm), lambda i: (0, i)),
            pl.BlockSpec(counts.shape, const),
        ],
        out_specs=[
            pl.BlockSpec((kk, tm), lambda i: (0, i)),
            pl.BlockSpec((2, nblk_pad), const),
            pl.BlockSpec((8, LANES), const),
        ],
        out_shape=[
            jax.ShapeDtypeStruct((kk, t), I32),
            jax.ShapeDtypeStruct((2, nblk_pad), I32),
            jax.ShapeDtypeStruct((8, LANES), I32),
        ],
        compiler_params=_params("arbitrary"),
        name="plan",
    )(top_i, rank, counts)


def _slab_rows(row, n):
    start = row * ROW_SLABS
    if not isinstance(start, int):
        start = pl.multiple_of(start, ROW_SLABS)
    return pl.ds(start, n * ROW_SLABS)


def _rows_copy(src_ref, src_row, dst_ref, dst_row, sem, n=1):
    return pltpu.make_async_copy(src_ref.at[_slab_rows(src_row, n)], dst_ref.at[_slab_rows(dst_row, n)], sem)


def _dispatch_kernel(meta_ref, pos_ref, h_ref, xr_ref, zeros_ref, hbuf, fill_sem, row_sem, load_sems):
    i = pl.program_id(0)
    last = pl.num_programs(0) - 1
    tm = pos_ref.shape[1]
    blk = EXPERT_ROWS
    n_blocks = xr_ref.shape[0] // (blk * ROW_SLABS)

    def fill(dst_row):
        return _rows_copy(zeros_ref, 0, xr_ref, dst_row, fill_sem, blk)

    def owns_rows(e):
        return meta_ref[2, e] > 0

    @pl.when(i == 0)
    def _():
        zeros_ref[...] = jnp.zeros_like(zeros_ref)
        for e in range(N_EXPERTS):
            @pl.when(owns_rows(e))
            def _():
                fill(meta_ref[0, e] - blk).start()

        def tail_start(b, carry):
            fill(b * blk).start()
            return carry

        def tail_wait(b, carry):
            fill(b * blk).wait()
            return carry

        lax.fori_loop(meta_ref[1, 0], n_blocks, tail_start, 0)
        for e in range(N_EXPERTS):
            @pl.when(owns_rows(e))
            def _():
                fill(meta_ref[0, e] - blk).wait()
        lax.fori_loop(meta_ref[1, 0], n_blocks, tail_wait, 0)

    slot = i % 2

    def load(step, dst_slot):
        return pltpu.make_async_copy(h_ref.at[_slab_rows(step * tm, tm)], hbuf.at[dst_slot],
                                     load_sems.at[dst_slot])

    def issue(g, carry):
        for u in range(MOVE_UNROLL):
            t = g * MOVE_UNROLL + u
            for k in range(TOPK_EXPERTS):
                _rows_copy(hbuf.at[slot], t, xr_ref, pos_ref[k, t], row_sem).start(priority=(u + k) % 2)
        return carry

    def drain(g, carry):
        for u in range(MOVE_UNROLL * TOPK_EXPERTS):
            _rows_copy(hbuf.at[0], 0, xr_ref, 0, row_sem).wait()
        return carry

    @pl.when(i == 0)
    def _():
        load(0, 0).start()

    @pl.when(i > 0)
    def _():
        lax.fori_loop(0, tm // MOVE_UNROLL, drain, 0)

    @pl.when(i < last)
    def _():
        load(i + 1, 1 - slot).start()

    load(i, slot).wait()
    lax.fori_loop(0, tm // MOVE_UNROLL, issue, 0)

    @pl.when(i == last)
    def _():
        lax.fori_loop(0, tm // MOVE_UNROLL, drain, 0)


def _dispatch(meta, pos, h2_tiles, n_rows):
    t = h2_tiles.shape[0] // ROW_SLABS
    tm = MOVE_ROWS
    grid_spec = pltpu.PrefetchScalarGridSpec(
        num_scalar_prefetch=1,
        grid=(t // tm,),
        in_specs=[
            pl.BlockSpec((TOPK_EXPERTS, tm), lambda i, meta: (0, i), memory_space=pltpu.SMEM),
            pl.BlockSpec(memory_space=pl.ANY),
        ],
        out_specs=pl.BlockSpec(memory_space=pl.ANY),
        scratch_shapes=[pltpu.VMEM((EXPERT_ROWS * ROW_SLABS, LANES), F32),
                        pltpu.VMEM((2, tm * ROW_SLABS, LANES), F32),
                        pltpu.SemaphoreType.DMA(()), pltpu.SemaphoreType.DMA(()),
                        pltpu.SemaphoreType.DMA((2,))],
    )
    return pl.pallas_call(
        _dispatch_kernel,
        grid_spec=grid_spec,
        out_shape=jax.ShapeDtypeStruct((n_rows * ROW_SLABS, LANES), F32),
        compiler_params=_params("arbitrary"),
        name="dispatch",
    )(meta, pos, h2_tiles)


def _expert_kernel(be_ref, meta_ref, x_ref, wgu_ref, bgu_ref, wdn_ref, bdn_ref, y_ref,
                   wgu_stage, wdn_stage, wgu_bf, wdn_bf, sems, *, layer):
    i = pl.program_id(0)
    ff = wdn_bf.shape[0]
    n_used = meta_ref[1, 0]

    def fetch(e):
        return (pltpu.make_async_copy(wgu_ref.at[layer, e], wgu_stage, sems.at[0]),
                pltpu.make_async_copy(wdn_ref.at[layer, e], wdn_stage, sems.at[1]))

    @pl.when(i >= n_used)
    def _():
        y_ref[...] = jnp.zeros_like(y_ref)

    @pl.when(i < n_used)
    def _():
        cur = be_ref[0, i]
        prev = be_ref[0, jnp.maximum(i - 1, 0)]

        @pl.when(i == 0)
        def _():
            for cp in fetch(cur):
                cp.start()

        @pl.when((i == 0) | (cur != prev))
        def _():
            for cp in fetch(cur):
                cp.wait()
            wgu_bf[...] = wgu_stage[...].astype(BF16)
            wdn_bf[...] = wdn_stage[...].astype(BF16)
            nxt = be_ref[1, i]

            @pl.when(nxt < N_EXPERTS)
            def _():
                for cp in fetch(nxt):
                    cp.start()

        x = _load_row_tiles(x_ref, EXPERT_ROWS, ROW_SLABS).astype(BF16)
        gu = _dot(x, wgu_bf[...]) + bgu_ref[0, 0]
        gate = jnp.minimum(gu[:, :ff], SWIGLU_LIMIT)
        up = jnp.clip(gu[:, ff:], -SWIGLU_LIMIT, SWIGLU_LIMIT)
        act = (up + 1.0) * gate * jax.nn.sigmoid(SWIGLU_ALPHA * gate)
        _store_row_tiles(y_ref, _dot(act.astype(BF16), wdn_bf[...]) + bdn_ref[0, 0])


def _experts(layer, blk_e, meta, xr, w_gu, b_gu, w_dn, b_dn):
    n_rows = xr.shape[0] // ROW_SLABS
    d = ROW_SLABS * LANES
    ff2 = w_gu.shape[-1]
    ff = ff2 // 2
    tm = EXPERT_ROWS
    nblk = n_rows // tm

    def last_used(i, meta):
        return jnp.minimum(i, meta[1, 0] - 1)

    xmap = lambda i, be, meta: (last_used(i, meta), 0)
    wmap = lambda i, be, meta: (layer, be[0, last_used(i, meta)], 0, 0)
    grid_spec = pltpu.PrefetchScalarGridSpec(
        num_scalar_prefetch=2,
        grid=(nblk,),
        in_specs=[
            pl.BlockSpec((tm * ROW_SLABS, LANES), xmap),
            pl.BlockSpec(memory_space=pl.ANY),
            pl.BlockSpec((1, 1, 1, ff2), wmap),
            pl.BlockSpec(memory_space=pl.ANY),
            pl.BlockSpec((1, 1, 1, d), wmap),
        ],
        out_specs=pl.BlockSpec((tm * ROW_SLABS, LANES), lambda i, be, meta: (i, 0)),
        scratch_shapes=[pltpu.VMEM((d, ff2), F32), pltpu.VMEM((ff, d), F32),
                        pltpu.VMEM((d, ff2), BF16), pltpu.VMEM((ff, d), BF16),
                        pltpu.SemaphoreType.DMA((2,))],
    )
    return pl.pallas_call(
        functools.partial(_expert_kernel, layer=layer),
        grid_spec=grid_spec,
        out_shape=jax.ShapeDtypeStruct((n_rows * ROW_SLABS, LANES), F32),
        compiler_params=_params("arbitrary"),
        name="experts",
    )(blk_e, meta, xr, w_gu, b_gu, w_dn, b_dn)


def _combine_kernel(pos_ref, nxt_ref, x1_ref, g_ref, g2_ref, fw_ref, yr_ref, o_ref, buf, sems, *, final_norm):
    i = pl.program_id(0)
    n = pl.num_programs(0)
    tm = x1_ref.shape[0]
    slot = i % 2

    def gather(table_ref, dst_slot):
        def issue(g, carry):
            for u in range(MOVE_UNROLL):
                t = g * MOVE_UNROLL + u
                for k in range(TOPK_EXPERTS):
                    _rows_copy(yr_ref, table_ref[k, t], buf.at[dst_slot, k], t,
                               sems.at[dst_slot]).start(priority=(u + k) % 2)
            return carry
        lax.fori_loop(0, tm // MOVE_UNROLL, issue, 0)

    def drain(g, carry):
        for u in range(MOVE_UNROLL * TOPK_EXPERTS):
            _rows_copy(yr_ref, 0, buf.at[slot, 0], 0, sems.at[slot]).wait()
        return carry

    @pl.when(i == 0)
    def _():
        gather(pos_ref, 0)

    @pl.when(i + 1 < n)
    def _():
        gather(nxt_ref, 1 - slot)

    lax.fori_loop(0, tm // MOVE_UNROLL, drain, 0)
    acc = g_ref[:, 0:1] * _load_row_tiles(buf.at[slot, 0], tm, ROW_SLABS)
    for k in range(1, TOPK_EXPERTS):
        acc = acc + g_ref[:, k:k + 1] * _load_row_tiles(buf.at[slot, k], tm, ROW_SLABS)
    x2 = x1_ref[...] + g2_ref[0, 0] * acc
    if final_norm:
        ms = jnp.mean(x2 * x2, axis=-1, keepdims=True)
        x2 = x2 * lax.rsqrt(ms + NORM_EPS) * fw_ref[...]
    o_ref[...] = x2


def _combine(layer, pos, x1, gates, mod, final_w, yr, seq, final_norm):
    t, d = x1.shape
    tm = MOVE_ROWS
    per_b = seq // tm
    kk = TOPK_EXPERTS
    n = t // tm
    return pl.pallas_call(
        functools.partial(_combine_kernel, final_norm=final_norm),
        grid=(n,),
        in_specs=[
            pl.BlockSpec((kk, tm), lambda i: (0, i), memory_space=pltpu.SMEM),
            pl.BlockSpec((kk, tm), lambda i: (0, jnp.minimum(i + 1, n - 1)), memory_space=pltpu.SMEM),
            pl.BlockSpec((tm, d), lambda i: (i, 0)),
            pl.BlockSpec((tm, kk), lambda i: (i, 0)),
            _mod_spec(layer, 5, d, per_b),
            pl.BlockSpec((1, d), lambda i: (0, 0)),
            pl.BlockSpec(memory_space=pl.ANY),
        ],
        out_specs=pl.BlockSpec((tm, d), lambda i: (i, 0)),
        out_shape=jax.ShapeDtypeStruct((t, d), F32),
        scratch_shapes=[pltpu.VMEM((2, kk, tm * ROW_SLABS, LANES), F32), pltpu.SemaphoreType.DMA((2,))],
        compiler_params=_params("arbitrary"),
        name="combine",
    )(pos, pos, x1, gates, mod, final_w, yr)


def kernel(x, c, ada_w, ada_b, norm_mix_w, w_in, ret_norm_w, lru_conv_w, lru_conv_b, lru_gate_a_w,
           lru_gate_a_b, lru_gate_x_w, lru_gate_x_b, lru_lambda, w_out, norm_ffn_w, router_w, router_b,
           moe_w_gu, moe_b_gu, moe_w_down, moe_b_down, final_norm_w):
    batch, seq, d = x.shape
    assert d == ROW_SLABS * LANES, "MoE row buffers store one (8, 128) tile per token row"
    depth = ada_w.shape[0]
    t = batch * seq
    ne = N_EXPERTS
    n_blocks = (t * TOPK_EXPERTS) // EXPERT_ROWS + ne
    tables = _retention_tables(seq)
    mod = _ada_mod(c, ada_w, ada_b).reshape(depth, batch, 1, N_MOD * d)
    w_in_bf = w_in.astype(BF16)
    w_out_bf = w_out.astype(BF16)
    wa_bd = _block_diag(lru_gate_a_w).astype(BF16)
    wx_bd = _block_diag(lru_gate_x_w).astype(BF16)
    rw_t = jnp.swapaxes(router_w, 1, 2)
    vec = lambda p: p.reshape(depth, 1, p.shape[-1])
    b_gu = moe_b_gu.reshape(depth, ne, 1, -1)
    b_dn = moe_b_down.reshape(depth, ne, 1, d)
    xf = x.reshape(t, d)
    for l in range(depth):
        ret, moba, lru = _inproj(l, xf, vec(norm_mix_w), mod, w_in_bf, seq)
        y_ret = _retention(l, ret, vec(ret_norm_w), tables, batch, seq)
        y_moba = _moba(moba, batch, seq)
        y_lru = _lru(l, lru, lru_conv_w, vec(lru_conv_b), wa_bd, vec(lru_gate_a_b), wx_bd,
                     vec(lru_gate_x_b), vec(lru_lambda), batch, seq)
        x1, h2, top_i, gates, rank, counts = _outproj_router(
            l, xf, y_ret, y_moba, y_lru, w_out_bf, mod, vec(norm_ffn_w), rw_t,
            router_b.reshape(depth, ne, 1), seq)
        pos, blk_e, meta = _plan(top_i, rank, counts, n_blocks)
        xr = _dispatch(meta, pos, h2, n_blocks * EXPERT_ROWS)
        yr = _experts(l, blk_e, meta, xr, moe_w_gu, b_gu, moe_w_down, b_dn)
        xf = _combine(l, pos, x1, gates.T, mod, final_norm_w.reshape(1, d), yr, seq,
                      final_norm=(l == depth - 1))
    return xf.reshape(batch, seq, d)
```

```python
import functools

import jax
import jax.numpy as jnp
from jax import lax
from jax.experimental import pallas as pl
from jax.experimental.pallas import tpu as pltpu

F32 = jnp.float32
BF16 = jnp.bfloat16
I32 = jnp.int32

HEAD_DIM = 64
RET_HEADS = 4
RET_W = RET_HEADS * HEAD_DIM
RET_CHUNK = 128
ROPE_BASE = 10000.0
MOBA_HEADS = 6
MOBA_W = MOBA_HEADS * HEAD_DIM
MOBA_BLOCK = 256
MOBA_TOPK = 3
LRU_BLOCKS = 6
LRU_BLOCK_W = 64
LRU_W = LRU_BLOCKS * LRU_BLOCK_W
CONV_WIDTH = 4
LRU_C = 8.0
N_EXPERTS = 32
TOPK_EXPERTS = 4
SWIGLU_LIMIT = 7.0
SWIGLU_ALPHA = 1.702
NORM_EPS = 1e-6
N_MOD = 6

LANES = 128
ROW_SLABS = 8
VMEM_LIMIT = 56 * 1024 * 1024
ROW_TILE = 512
EXPERT_ROWS = 256
MOVE_ROWS = 256
MOVE_UNROLL = 8
PLAN_ROWS = 2048
RET_SEQS = 2
LRU_GROUP = 8

NT_DIMS = (((1,), (1,)), ((), ()))
NN_DIMS = (((1,), (0,)), ((), ()))
TN_DIMS = (((0,), (0,)), ((), ()))


def _params(*semantics):
    return pltpu.CompilerParams(dimension_semantics=semantics, vmem_limit_bytes=VMEM_LIMIT)


def _dot(a, b, dims=NN_DIMS):
    return lax.dot_general(a, b, dims, preferred_element_type=F32)


def _split_bf16(a):
    hi = a.astype(BF16)
    lo = (a - hi.astype(F32)).astype(BF16)
    return hi, lo


def _dot3(a, b, dims=NN_DIMS):
    ah, al = _split_bf16(a)
    bh, bl = _split_bf16(b)
    return _dot(ah, bl, dims) + _dot(al, bh, dims) + _dot(ah, bh, dims)


def _rms_mod(x, nw, scale, shift):
    ms = jnp.mean(x * x, axis=-1, keepdims=True)
    y = x * lax.rsqrt(ms + NORM_EPS) * nw
    return y * (1.0 + scale) + shift


def _store_row_tiles(dst_ref, x):
    n, d = x.shape
    nc = d // LANES
    for c in range(nc):
        dst_ref[pl.ds(c, n, stride=nc), :] = x[:, c * LANES:(c + 1) * LANES]


def _load_row_tiles(src_ref, n, nc):
    return jnp.concatenate([src_ref[pl.ds(c, n, stride=nc), :] for c in range(nc)], axis=1)


def _mod_spec(layer, chunk, d, per_b):
    return pl.BlockSpec((1, 1, 1, d), lambda i: (layer, i // per_b, 0, chunk))


def _layer_spec(layer, shape):
    zeros = (0,) * len(shape)
    return pl.BlockSpec((1,) + tuple(shape), lambda *_: (layer,) + zeros)


def _ada_kernel(c_ref, w_ref, b_ref, o_ref):
    c = c_ref[...]
    c_act = c * jax.nn.sigmoid(c)
    o_ref[0] = _dot3(c_act, w_ref[0]) + b_ref[0]


def _ada_mod(c, ada_w, ada_b):
    depth, d, n = ada_w.shape
    b = c.shape[0]
    tn = n // 4
    return pl.pallas_call(
        _ada_kernel,
        grid=(depth, n // tn),
        in_specs=[
            pl.BlockSpec((b, d), lambda l, j: (0, 0)),
            pl.BlockSpec((1, d, tn), lambda l, j: (l, 0, j)),
            pl.BlockSpec((1, 1, tn), lambda l, j: (l, 0, j)),
        ],
        out_specs=pl.BlockSpec((1, b, tn), lambda l, j: (l, 0, j)),
        out_shape=jax.ShapeDtypeStruct((depth, b, n), F32),
        compiler_params=_params("arbitrary", "arbitrary"),
        name="ada_mod",
    )(c, ada_w, ada_b.reshape(depth, 1, n))


def _inproj_kernel(x_ref, nw_ref, sc_ref, sh_ref, w_ref, ret_ref, moba_ref, lru_ref):
    h = _rms_mod(x_ref[...], nw_ref[0], sc_ref[0, 0], sh_ref[0, 0]).astype(BF16)
    o0 = RET_W * 4
    o1 = o0 + MOBA_W * 3
    ret_ref[...] = _dot(h, w_ref[0, :, 0:o0])
    moba_ref[...] = _dot(h, w_ref[0, :, o0:o1])
    lru_ref[...] = _dot(h, w_ref[0, :, o1:])


def _inproj(layer, x2d, norm_w, mod, w_bf16, seq):
    t, d = x2d.shape
    tm = ROW_TILE
    per_b = seq // tm
    n_ret, n_moba, n_lru = RET_W * 4, MOBA_W * 3, LRU_W * 2
    return pl.pallas_call(
        _inproj_kernel,
        grid=(t // tm,),
        in_specs=[
            pl.BlockSpec((tm, d), lambda i: (i, 0)),
            _layer_spec(layer, (1, d)),
            _mod_spec(layer, 1, d, per_b),
            _mod_spec(layer, 0, d, per_b),
            _layer_spec(layer, w_bf16.shape[1:]),
        ],
        out_specs=[
            pl.BlockSpec((tm, n_ret), lambda i: (i, 0)),
            pl.BlockSpec((tm, n_moba), lambda i: (i, 0)),
            pl.BlockSpec((tm, n_lru), lambda i: (i, 0)),
        ],
        out_shape=[
            jax.ShapeDtypeStruct((t, n_ret), F32),
            jax.ShapeDtypeStruct((t, n_moba), F32),
            jax.ShapeDtypeStruct((t, n_lru), F32),
        ],
        compiler_params=_params("arbitrary"),
        name="inproj",
    )(x2d, norm_w, mod, mod, w_bf16)


def _retention_tables(seq):
    h, dh, c = RET_HEADS, HEAD_DIM, RET_CHUNK
    half = dh // 2
    inv = 1.0 / (ROPE_BASE ** (jnp.arange(half, dtype=F32) / half))
    ang = jnp.arange(seq).astype(F32)[:, None] * inv[None, :]
    cos, sin = jnp.cos(ang), jnp.sin(ang)
    cos_t = jnp.tile(jnp.concatenate([cos, cos], axis=-1), (1, h))
    sin_t = jnp.tile(jnp.concatenate([-sin, sin], axis=-1), (1, h))
    log_g = jnp.log1p(-jnp.exp2(-5.0 - jnp.arange(h, dtype=F32)))
    idx = jnp.arange(c, dtype=F32)
    diff = idx[:, None] - idx[None, :]
    dmat = jnp.where(diff >= 0, jnp.exp(log_g[:, None, None] * jnp.maximum(diff, 0.0)), 0.0)
    zeta = jnp.exp(log_g[:, None] * (c - 1 - idx)[None, :])
    xi = jnp.exp(log_g[:, None] * (idx + 1)[None, :])
    zeta_t = jnp.repeat(zeta.T, dh, axis=1)
    xi_t = jnp.repeat(xi.T, dh, axis=1)
    decay = jnp.broadcast_to(jnp.exp(log_g * c)[:, None, None], (h, dh, dh))
    return cos_t, sin_t, dmat, zeta_t, xi_t, decay


def _retention_kernel(p_ref, cos_ref, sin_ref, dmat_ref, zeta_ref, xi_ref, dec_ref, nw_ref, o_ref):
    seq = cos_ref.shape[0]
    nseq = p_ref.shape[0] // seq
    c, dh, w = RET_CHUNK, HEAD_DIM, RET_W
    lane = lax.broadcasted_iota(I32, (1, w), 1)
    first_half = (lane % dh) < (dh // 2)

    def rope(x, cos, sin):
        partner = jnp.where(first_half, pltpu.roll(x, w - dh // 2, 1), pltpu.roll(x, dh // 2, 1))
        return x * cos + partner * sin

    def chunk(n, states):
        r0 = pl.multiple_of(n * c, c)
        cos, sin = cos_ref[pl.ds(r0, c), :], sin_ref[pl.ds(r0, c), :]
        new_states = []
        for b in range(nseq):
            rows = pl.ds(pl.multiple_of(b * seq + r0, c), c)
            q = rope(p_ref[rows, 0:w], cos, sin)
            k = rope(p_ref[rows, w:2 * w], cos, sin) * (dh ** -0.5)
            v = p_ref[rows, 2 * w:3 * w]
            g = p_ref[rows, 3 * w:4 * w]
            kz = k * zeta_ref[...]
            gate = g * jax.nn.sigmoid(g) * nw_ref[0]
            for hd in range(RET_HEADS):
                state = states[b * RET_HEADS + hd]
                cols = slice(hd * dh, (hd + 1) * dh)
                qh = q[:, cols].astype(BF16)
                kh = k[:, cols].astype(BF16)
                vh = v[:, cols].astype(BF16)
                scores = _dot(qh, kh, NT_DIMS) * dmat_ref[hd]
                intra = _dot(scores.astype(BF16), vh)
                cross = _dot(qh, state.astype(BF16)) * xi_ref[:, cols]
                kv = _dot(kz[:, cols].astype(BF16), vh, TN_DIMS)
                y = intra + cross
                mu = jnp.mean(y, axis=-1, keepdims=True)
                yc = y - mu
                var = jnp.mean(yc * yc, axis=-1, keepdims=True)
                o_ref[rows, cols] = yc * lax.rsqrt(var + NORM_EPS) * gate[:, cols]
                new_states.append(state * dec_ref[hd] + kv)
        return tuple(new_states)

    init = tuple(jnp.zeros((dh, dh), F32) for _ in range(nseq * RET_HEADS))
    lax.fori_loop(0, seq // c, chunk, init)


def _retention(layer, ret, norm_w, tables, batch, seq):
    cos_t, sin_t, dmat, zeta_t, xi_t, decay = tables
    w = RET_W
    const2 = lambda b: (0, 0)
    const3 = lambda b: (0, 0, 0)
    nseq = RET_SEQS if batch % RET_SEQS == 0 else 1
    return pl.pallas_call(
        _retention_kernel,
        grid=(batch // nseq,),
        in_specs=[
            pl.BlockSpec((nseq * seq, 4 * w), lambda b: (b, 0)),
            pl.BlockSpec((seq, w), const2),
            pl.BlockSpec((seq, w), const2),
            pl.BlockSpec(dmat.shape, const3),
            pl.BlockSpec(zeta_t.shape, const2),
            pl.BlockSpec(xi_t.shape, const2),
            pl.BlockSpec(decay.shape, const3),
            _layer_spec(layer, (1, w)),
        ],
        out_specs=pl.BlockSpec((nseq * seq, w), lambda b: (b, 0)),
        out_shape=jax.ShapeDtypeStruct((batch * seq, w), F32),
        compiler_params=_params("arbitrary"),
        name="retention",
    )(ret, cos_t, sin_t, dmat, zeta_t, xi_t, decay, norm_w)


def _moba_kernel(q_ref, k_ref, v_ref, o_ref):
    seq = q_ref.shape[0]
    bk, dh = MOBA_BLOCK, HEAD_DIM
    nb = seq // bk
    scale = dh ** -0.5
    neg = -jnp.inf
    lane = lax.broadcasted_iota(I32, (1, LANES), 1)
    k_all = k_ref[...]
    k_bf = k_all.astype(BF16)
    v_bf = v_ref[...].astype(BF16)
    kmean = jnp.mean(k_all.reshape(nb, bk, LANES), axis=1)
    colid = lax.broadcasted_iota(I32, (bk, nb), 1)
    causal = (lax.broadcasted_iota(I32, (bk, bk), 0)
              >= lax.broadcasted_iota(I32, (bk, bk), 1))
    heads_per_step = LANES // dh
    outs = [None] * nb
    for p in range(heads_per_step):
        head_lanes = (lane >= p * dh) & (lane < (p + 1) * dh)
        qm = jnp.where(head_lanes, q_ref[...], 0.0)
        gs = _dot3(qm, kmean, NT_DIMS)
        qm_bf = (qm * scale).astype(BF16)
        for i in range(nb):
            rows = slice(i * bk, (i + 1) * bk)
            klen = (i + 1) * bk
            s = _dot(qm_bf[rows], k_bf[:klen], NT_DIMS)
            pieces = []
            if i > MOBA_TOPK:
                gsi = gs[rows]
                rank = jnp.zeros((bk, nb), F32)
                for j2 in range(i):
                    cj = gsi[:, j2:j2 + 1]
                    tie = jnp.where(colid > j2, 1.0, 0.0)
                    rank = rank + jnp.where(cj > gsi, 1.0, jnp.where(cj == gsi, tie, 0.0))
                for j in range(i):
                    keep = rank[:, j:j + 1] < float(MOBA_TOPK)
                    pieces.append(jnp.where(keep, s[:, j * bk:(j + 1) * bk], neg))
            else:
                for j in range(i):
                    pieces.append(s[:, j * bk:(j + 1) * bk])
            pieces.append(jnp.where(causal, s[:, i * bk:klen], neg))
            m = pieces[0]
            for pc in pieces[1:]:
                m = jnp.maximum(m, pc)
            m = jnp.max(m, axis=-1, keepdims=True)
            probs = [jnp.exp(pc - m) for pc in pieces]
            tot = probs[0]
            for pr in probs[1:]:
                tot = tot + pr
            denom = jnp.sum(tot, axis=-1, keepdims=True)
            pcat = probs[0] if len(probs) == 1 else jnp.concatenate(probs, axis=1)
            o = _dot(pcat.astype(BF16), v_bf[:klen]) / denom
            outs[i] = o if outs[i] is None else jnp.where(head_lanes, o, outs[i])
    for i in range(nb):
        o_ref[i * bk:(i + 1) * bk, :] = outs[i]


def _moba(moba, batch, seq):
    npair = MOBA_W // LANES
    return pl.pallas_call(
        _moba_kernel,
        grid=(batch, npair),
        in_specs=[
            pl.BlockSpec((seq, LANES), lambda b, p: (b, p)),
            pl.BlockSpec((seq, LANES), lambda b, p: (b, npair + p)),
            pl.BlockSpec((seq, LANES), lambda b, p: (b, 2 * npair + p)),
        ],
        out_specs=pl.BlockSpec((seq, LANES), lambda b, p: (b, p)),
        out_shape=jax.ShapeDtypeStruct((batch * seq, MOBA_W), F32),
        compiler_params=_params("arbitrary", "arbitrary"),
        name="moba",
    )(moba, moba, moba)


def _lru_kernel(xg_ref, cw_ref, cb_ref, wa_ref, ba_ref, wx_ref, bx_ref, lam_ref, o_ref, a_scr, h_scr):
    seq = xg_ref.shape[0]
    w = LRU_W
    xb = xg_ref[:, 0:w]
    row = lax.broadcasted_iota(I32, (seq, w), 0)
    u = cb_ref[0] + cw_ref[0, CONV_WIDTH - 1:CONV_WIDTH, :] * xb
    for tap in range(CONV_WIDTH - 1):
        back = CONV_WIDTH - 1 - tap
        u = u + cw_ref[0, tap:tap + 1, :] * jnp.where(row >= back, pltpu.roll(xb, back, 0), 0.0)
    u_bf = u.astype(BF16)
    r = jax.nn.sigmoid(_dot(u_bf, wa_ref[0]) + ba_ref[0])
    ig = jax.nn.sigmoid(_dot(u_bf, wx_ref[0]) + bx_ref[0])
    z = -lam_ref[0]
    softplus = jnp.maximum(z, 0.0) + jnp.log1p(jnp.exp(-jnp.abs(z)))
    log_a = -LRU_C * r * softplus
    a = jnp.exp(log_a)
    b = jnp.sqrt(-jnp.tanh(log_a) * (a * a + 1.0)) * (ig * u)
    sub = row & (LRU_GROUP - 1)
    d = 1
    while d < LRU_GROUP:
        live = sub >= d
        a_prev = jnp.where(live, pltpu.roll(a, d, 0), 1.0)
        b_prev = jnp.where(live, pltpu.roll(b, d, 0), 0.0)
        b = a * b_prev + b
        a = a * a_prev
        d *= 2
    a_scr[...] = a
    h_scr[...] = b

    def group(gi, carry):
        rows = pl.ds(pl.multiple_of(gi * LRU_GROUP, LRU_GROUP), LRU_GROUP)
        h = a_scr[rows, :] * carry + h_scr[rows, :]
        h_scr[rows, :] = h
        return h[LRU_GROUP - 1:LRU_GROUP, :]

    lax.fori_loop(0, seq // LRU_GROUP, group, jnp.zeros((1, w), F32), unroll=8)
    o_ref[...] = jax.nn.gelu(xg_ref[:, w:2 * w]) * h_scr[...]


def _block_diag(wb):
    depth, g, n, _ = wb.shape
    eye = jnp.eye(g, dtype=wb.dtype)
    return (eye[None, :, None, :, None] * wb[:, :, :, None, :]).reshape(depth, g * n, g * n)


def _lru(layer, lru, cw, cb, wa_bd, ba, wx_bd, bx, lam, batch, seq):
    w = LRU_W
    vec = _layer_spec(layer, (1, w))
    mat = _layer_spec(layer, (w, w))
    return pl.pallas_call(
        _lru_kernel,
        grid=(batch,),
        in_specs=[
            pl.BlockSpec((seq, 2 * w), lambda b: (b, 0)),
            _layer_spec(layer, (CONV_WIDTH, w)),
            vec, mat, vec, mat, vec, vec,
        ],
        out_specs=pl.BlockSpec((seq, w), lambda b: (b, 0)),
        out_shape=jax.ShapeDtypeStruct((batch * seq, w), F32),
        scratch_shapes=[pltpu.VMEM((seq, w), F32), pltpu.VMEM((seq, w), F32)],
        compiler_params=_params("arbitrary"),
        name="rg_lru",
    )(lru, cw, cb, wa_bd, ba, wx_bd, bx, lam)


def _outproj_router_kernel(x_ref, yr_ref, ym_ref, yl_ref, wo_ref, g1_ref, nw_ref,
                           sc_ref, sh_ref, rw_ref, rb_ref,
                           x1_ref, h2_ref, ti_ref, tg_ref, rk_ref, cnt_ref, carry_ref):
    i = pl.program_id(0)
    tm = x_ref.shape[0]
    ne = N_EXPERTS
    o0, o1 = RET_W, RET_W + MOBA_W

    @pl.when(i == 0)
    def _():
        carry_ref[...] = jnp.zeros_like(carry_ref)

    mixed = (_dot(yr_ref[...].astype(BF16), wo_ref[0, 0:o0, :])
             + _dot(ym_ref[...].astype(BF16), wo_ref[0, o0:o1, :])
             + _dot(yl_ref[...].astype(BF16), wo_ref[0, o1:, :]))
    x1 = x_ref[...] + g1_ref[0, 0] * mixed
    x1_ref[...] = x1
    h2 = _rms_mod(x1, nw_ref[0], sc_ref[0, 0], sh_ref[0, 0])
    _store_row_tiles(h2_ref, h2)

    logits = _dot3(rw_ref[0], h2, NT_DIMS) + rb_ref[0]
    eid = lax.broadcasted_iota(I32, (ne, tm), 0)
    work = logits
    member = jnp.zeros((ne, tm), F32)
    vals, hots = [], []
    for k in range(TOPK_EXPERTS):
        m = jnp.max(work, axis=0, keepdims=True)
        idx = jnp.min(jnp.where(work == m, eid, ne), axis=0, keepdims=True)
        hot = eid == idx
        ti_ref[k:k + 1, :] = idx
        vals.append(m)
        hots.append(hot)
        member = member + jnp.where(hot, 1.0, 0.0)
        work = jnp.where(hot, -jnp.inf, work)
    exps = [jnp.exp(v - vals[0]) for v in vals]
    tot = exps[0]
    for e in exps[1:]:
        tot = tot + e
    for k in range(TOPK_EXPERTS):
        tg_ref[k:k + 1, :] = exps[k] / tot

    before = (lax.broadcasted_iota(I32, (tm, tm), 0)
              < lax.broadcasted_iota(I32, (tm, tm), 1))
    prefix = _dot(member.astype(BF16), jnp.where(before, 1.0, 0.0).astype(BF16))
    prefix = prefix + carry_ref[:, 0:1]
    for k in range(TOPK_EXPERTS):
        rk = jnp.sum(jnp.where(hots[k], prefix, 0.0), axis=0, keepdims=True)
        rk_ref[k:k + 1, :] = rk.astype(I32)
    carry_ref[...] = carry_ref[...] + jnp.sum(member, axis=1, keepdims=True)
    cnt_ref[...] = carry_ref[...].astype(I32)


def _outproj_router(layer, x2d, y_ret, y_moba, y_lru, wo_bf16, mod, norm_w, rw_t, rb, seq):
    t, d = x2d.shape
    tm = ROW_TILE
    per_b = seq // tm
    ne, kk = N_EXPERTS, TOPK_EXPERTS
    const = lambda i: (0, 0)
    row = lambda n: pl.BlockSpec((tm, n), lambda i: (i, 0))
    lanes = pl.BlockSpec((kk, tm), lambda i: (0, i))
    return pl.pallas_call(
        _outproj_router_kernel,
        grid=(t // tm,),
        in_specs=[
            row(d), row(RET_W), row(MOBA_W), row(LRU_W),
            _layer_spec(layer, wo_bf16.shape[1:]),
            _mod_spec(layer, 2, d, per_b),
            _layer_spec(layer, (1, d)),
            _mod_spec(layer, 4, d, per_b),
            _mod_spec(layer, 3, d, per_b),
            _layer_spec(layer, (ne, d)),
            _layer_spec(layer, (ne, 1)),
        ],
        out_specs=[
            row(d), pl.BlockSpec((tm * (d // LANES), LANES), lambda i: (i, 0)), lanes, lanes, lanes,
            pl.BlockSpec((ne, LANES), const),
        ],
        out_shape=[
            jax.ShapeDtypeStruct((t, d), F32),
            jax.ShapeDtypeStruct((t * (d // LANES), LANES), F32),
            jax.ShapeDtypeStruct((kk, t), I32),
            jax.ShapeDtypeStruct((kk, t), F32),
            jax.ShapeDtypeStruct((kk, t), I32),
            jax.ShapeDtypeStruct((ne, LANES), I32),
        ],
        scratch_shapes=[pltpu.VMEM((ne, LANES), F32)],
        compiler_params=_params("arbitrary"),
        name="outproj_router",
    )(x2d, y_ret, y_moba, y_lru, wo_bf16, mod, norm_w, mod, mod, rw_t, rb)


def _plan_kernel(ti_ref, rk_ref, cnt_ref, pos_ref, blk_ref, meta_ref):
    ne = N_EXPERTS
    tm = ti_ref.shape[1]
    nblk_pad = blk_ref.shape[1]
    shift = EXPERT_ROWS.bit_length() - 1
    counts = cnt_ref[...]
    padded = ((counts + (EXPERT_ROWS - 1)) >> shift) << shift
    row = lax.broadcasted_iota(I32, (ne, LANES), 0)
    pend = padded
    d = 1
    while d < ne:
        pend = pend + jnp.where(row >= d, pltpu.roll(pend, d, 0), 0)
        d *= 2
    pstart = (pend - padded)[:, 0:1]
    eid = lax.broadcasted_iota(I32, (ne, tm), 0)
    for k in range(TOPK_EXPERTS):
        base = jnp.sum(jnp.where(ti_ref[k:k + 1, :] == eid, pstart, 0), axis=0, keepdims=True)
        pos_ref[k:k + 1, :] = base + rk_ref[k:k + 1, :]
    starts = lax.broadcasted_iota(I32, (ne, nblk_pad), 1) * EXPERT_ROWS
    owner = jnp.sum(jnp.where(pend[:, 0:1] <= starts, 1, 0), axis=0, keepdims=True)
    owner = jnp.minimum(owner, ne - 1)
    blk_ref[0:1, :] = owner
    eid_blk = lax.broadcasted_iota(I32, (ne, nblk_pad), 0)
    later = (eid_blk > owner) & (padded[:, 0:1] > 0)
    blk_ref[1:2, :] = jnp.min(jnp.where(later, eid_blk, ne), axis=0, keepdims=True)
    lane = lax.broadcasted_iota(I32, (ne, LANES), 1)
    pend_lanes = jnp.sum(jnp.where(row == lane, pend, 0), axis=0, keepdims=True)
    total = jnp.max(pend, axis=0, keepdims=True)
    meta_ref[0:1, :] = pend_lanes
    meta_ref[1:2, :] = total >> shift
    meta_ref[2:3, :] = jnp.sum(jnp.where(row == lane, padded, 0), axis=0, keepdims=True)
    meta_ref[3:8, :] = jnp.zeros((5, LANES), I32)


def _plan(top_i, rank, counts, n_blocks):
    kk, t = top_i.shape
    tm = PLAN_ROWS
    nblk_pad = -(-n_blocks // LANES) * LANES
    const = lambda i: (0, 0)
    return pl.pallas_call(
        _plan_kernel,
        grid=(t // tm,),
        in_specs=[
            pl.BlockSpec((kk, tm), lambda i: (0, i)),
            pl.BlockSpec((kk, tm), lambda i: (0, i)),
            pl.BlockSpec(counts.shape, const),
        ],
        out_specs=[
            pl.BlockSpec((kk, tm), lambda i: (0, i)),
            pl.BlockSpec((2, nblk_pad), const),
            pl.BlockSpec((8, LANES), const),
        ],
        out_shape=[
            jax.ShapeDtypeStruct((kk, t), I32),
            jax.ShapeDtypeStruct((2, nblk_pad), I32),
            jax.ShapeDtypeStruct((8, LANES), I32),
        ],
        compiler_params=_params("arbitrary"),
        name="plan",
    )(top_i, rank, counts)


def _slab_rows(row, n):
    start = row * ROW_SLABS
    if not isinstance(start, int):
        start = pl.multiple_of(start, ROW_SLABS)
    return pl.ds(start, n * ROW_SLABS)


def _rows_copy(src_ref, src_row, dst_ref, dst_row, sem, n=1):
    return pltpu.make_async_copy(src_ref.at[_slab_rows(src_row, n)], dst_ref.at[_slab_rows(dst_row, n)], sem)


def _dispatch_kernel(meta_ref, pos_ref, h_ref, xr_ref, zeros_ref, hbuf, fill_sem, row_sem, load_sems):
    i = pl.program_id(0)
    last = pl.num_programs(0) - 1
    tm = pos_ref.shape[1]
    blk = EXPERT_ROWS
    n_blocks = xr_ref.shape[0] // (blk * ROW_SLABS)

    def fill(dst_row):
        return _rows_copy(zeros_ref, 0, xr_ref, dst_row, fill_sem, blk)

    def owns_rows(e):
        return meta_ref[2, e] > 0

    @pl.when(i == 0)
    def _():
        zeros_ref[...] = jnp.zeros_like(zeros_ref)
        for e in range(N_EXPERTS):
            @pl.when(owns_rows(e))
            def _():
                fill(meta_ref[0, e] - blk).start()

        def tail_start(b, carry):
            fill(b * blk).start()
            return carry

        def tail_wait(b, carry):
            fill(b * blk).wait()
            return carry

        lax.fori_loop(meta_ref[1, 0], n_blocks, tail_start, 0)
        for e in range(N_EXPERTS):
            @pl.when(owns_rows(e))
            def _():
                fill(meta_ref[0, e] - blk).wait()
        lax.fori_loop(meta_ref[1, 0], n_blocks, tail_wait, 0)

    slot = i % 2

    def load(step, dst_slot):
        return pltpu.make_async_copy(h_ref.at[_slab_rows(step * tm, tm)], hbuf.at[dst_slot],
                                     load_sems.at[dst_slot])

    def issue(g, carry):
        for u in range(MOVE_UNROLL):
            t = g * MOVE_UNROLL + u
            for k in range(TOPK_EXPERTS):
                _rows_copy(hbuf.at[slot], t, xr_ref, pos_ref[k, t], row_sem).start(priority=(u + k) % 2)
        return carry

    def drain(g, carry):
        for u in range(MOVE_UNROLL * TOPK_EXPERTS):
            _rows_copy(hbuf.at[0], 0, xr_ref, 0, row_sem).wait()
        return carry

    @pl.when(i == 0)
    def _():
        load(0, 0).start()

    @pl.when(i > 0)
    def _():
        lax.fori_loop(0, tm // MOVE_UNROLL, drain, 0)

    @pl.when(i < last)
    def _():
        load(i + 1, 1 - slot).start()

    load(i, slot).wait()
    lax.fori_loop(0, tm // MOVE_UNROLL, issue, 0)

    @pl.when(i == last)
    def _():
        lax.fori_loop(0, tm // MOVE_UNROLL, drain, 0)


def _dispatch(meta, pos, h2_tiles, n_rows):
    t = h2_tiles.shape[0] // ROW_SLABS
    tm = MOVE_ROWS
    grid_spec = pltpu.PrefetchScalarGridSpec(
        num_scalar_prefetch=1,
        grid=(t // tm,),
        in_specs=[
            pl.BlockSpec((TOPK_EXPERTS, tm), lambda i, meta: (0, i), memory_space=pltpu.SMEM),
            pl.BlockSpec(memory_space=pl.ANY),
        ],
        out_specs=pl.BlockSpec(memory_space=pl.ANY),
        scratch_shapes=[pltpu.VMEM((EXPERT_ROWS * ROW_SLABS, LANES), F32),
                        pltpu.VMEM((2, tm * ROW_SLABS, LANES), F32),
                        pltpu.SemaphoreType.DMA(()), pltpu.SemaphoreType.DMA(()),
                        pltpu.SemaphoreType.DMA((2,))],
    )
    return pl.pallas_call(
        _dispatch_kernel,
        grid_spec=grid_spec,
        out_shape=jax.ShapeDtypeStruct((n_rows * ROW_SLABS, LANES), F32),
        compiler_params=_params("arbitrary"),
        name="dispatch",
    )(meta, pos, h2_tiles)


def _expert_kernel(be_ref, meta_ref, x_ref, wgu_ref, bgu_ref, wdn_ref, bdn_ref, y_ref,
                   wgu_stage, wdn_stage, wgu_bf, wdn_bf, sems, *, layer):
    i = pl.program_id(0)
    ff = wdn_bf.shape[0]
    n_used = meta_ref[1, 0]

    def fetch(e):
        return (pltpu.make_async_copy(wgu_ref.at[layer, e], wgu_stage, sems.at[0]),
                pltpu.make_async_copy(wdn_ref.at[layer, e], wdn_stage, sems.at[1]))

    @pl.when(i >= n_used)
    def _():
        y_ref[...] = jnp.zeros_like(y_ref)

    @pl.when(i < n_used)
    def _():
        cur = be_ref[0, i]
        prev = be_ref[0, jnp.maximum(i - 1, 0)]

        @pl.when(i == 0)
        def _():
            for cp in fetch(cur):
                cp.start()

        @pl.when((i == 0) | (cur != prev))
        def _():
            for cp in fetch(cur):
                cp.wait()
            wgu_bf[...] = wgu_stage[...].astype(BF16)
            wdn_bf[...] = wdn_stage[...].astype(BF16)
            nxt = be_ref[1, i]

            @pl.when(nxt < N_EXPERTS)
            def _():
                for cp in fetch(nxt):
                    cp.start()

        x = _load_row_tiles(x_ref, EXPERT_ROWS, ROW_SLABS).astype(BF16)
        gu = _dot(x, wgu_bf[...]) + bgu_ref[0, 0]
        gate = jnp.minimum(gu[:, :ff], SWIGLU_LIMIT)
        up = jnp.clip(gu[:, ff:], -SWIGLU_LIMIT, SWIGLU_LIMIT)
        act = (up + 1.0) * gate * jax.nn.sigmoid(SWIGLU_ALPHA * gate)
        _store_row_tiles(y_ref, _dot(act.astype(BF16), wdn_bf[...]) + bdn_ref[0, 0])


def _experts(layer, blk_e, meta, xr, w_gu, b_gu, w_dn, b_dn):
    n_rows = xr.shape[0] // ROW_SLABS
    d = ROW_SLABS * LANES
    ff2 = w_gu.shape[-1]
    ff = ff2 // 2
    tm = EXPERT_ROWS
    nblk = n_rows // tm

    def last_used(i, meta):
        return jnp.minimum(i, meta[1, 0] - 1)

    xmap = lambda i, be, meta: (last_used(i, meta), 0)
    wmap = lambda i, be, meta: (layer, be[0, last_used(i, meta)], 0, 0)
    grid_spec = pltpu.PrefetchScalarGridSpec(
        num_scalar_prefetch=2,
        grid=(nblk,),
        in_specs=[
            pl.BlockSpec((tm * ROW_SLABS, LANES), xmap),
            pl.BlockSpec(memory_space=pl.ANY),
            pl.BlockSpec((1, 1, 1, ff2), wmap),
            pl.BlockSpec(memory_space=pl.ANY),
            pl.BlockSpec((1, 1, 1, d), wmap),
        ],
        out_specs=pl.BlockSpec((tm * ROW_SLABS, LANES), lambda i, be, meta: (i, 0)),
        scratch_shapes=[pltpu.VMEM((d, ff2), F32), pltpu.VMEM((ff, d), F32),
                        pltpu.VMEM((d, ff2), BF16), pltpu.VMEM((ff, d), BF16),
                        pltpu.SemaphoreType.DMA((2,))],
    )
    return pl.pallas_call(
        functools.partial(_expert_kernel, layer=layer),
        grid_spec=grid_spec,
        out_shape=jax.ShapeDtypeStruct((n_rows * ROW_SLABS, LANES), F32),
        compiler_params=_params("arbitrary"),
        name="experts",
    )(blk_e, meta, xr, w_gu, b_gu, w_dn, b_dn)


def _combine_kernel(pos_ref, nxt_ref, x1_ref, g_ref, g2_ref, fw_ref, yr_ref, o_ref, buf, sems, *, final_norm):
    i = pl.program_id(0)
    n = pl.num_programs(0)
    tm = x1_ref.shape[0]
    slot = i % 2

    def gather(table_ref, dst_slot):
        def issue(g, carry):
            for u in range(MOVE_UNROLL):
                t = g * MOVE_UNROLL + u
                for k in range(TOPK_EXPERTS):
                    _rows_copy(yr_ref, table_ref[k, t], buf.at[dst_slot, k], t,
                               sems.at[dst_slot]).start(priority=(u + k) % 2)
            return carry
        lax.fori_loop(0, tm // MOVE_UNROLL, issue, 0)

    def drain(g, carry):
        for u in range(MOVE_UNROLL * TOPK_EXPERTS):
            _rows_copy(yr_ref, 0, buf.at[slot, 0], 0, sems.at[slot]).wait()
        return carry

    @pl.when(i == 0)
    def _():
        gather(pos_ref, 0)

    @pl.when(i + 1 < n)
    def _():
        gather(nxt_ref, 1 - slot)

    lax.fori_loop(0, tm // MOVE_UNROLL, drain, 0)
    acc = g_ref[:, 0:1] * _load_row_tiles(buf.at[slot, 0], tm, ROW_SLABS)
    for k in range(1, TOPK_EXPERTS):
        acc = acc + g_ref[:, k:k + 1] * _load_row_tiles(buf.at[slot, k], tm, ROW_SLABS)
    x2 = x1_ref[...] + g2_ref[0, 0] * acc
    if final_norm:
        ms = jnp.mean(x2 * x2, axis=-1, keepdims=True)
        x2 = x2 * lax.rsqrt(ms + NORM_EPS) * fw_ref[...]
    o_ref[...] = x2


def _combine(layer, pos, x1, gates, mod, final_w, yr, seq, final_norm):
    t, d = x1.shape
    tm = MOVE_ROWS
    per_b = seq // tm
    kk = TOPK_EXPERTS
    n = t // tm
    return pl.pallas_call(
        functools.partial(_combine_kernel, final_norm=final_norm),
        grid=(n,),
        in_specs=[
            pl.BlockSpec((kk, tm), lambda i: (0, i), memory_space=pltpu.SMEM),
            pl.BlockSpec((kk, tm), lambda i: (0, jnp.minimum(i + 1, n - 1)), memory_space=pltpu.SMEM),
            pl.BlockSpec((tm, d), lambda i: (i, 0)),
            pl.BlockSpec((tm, kk), lambda i: (i, 0)),
            _mod_spec(layer, 5, d, per_b),
            pl.BlockSpec((1, d), lambda i: (0, 0)),
            pl.BlockSpec(memory_space=pl.ANY),
        ],
        out_specs=pl.BlockSpec((tm, d), lambda i: (i, 0)),
        out_shape=jax.ShapeDtypeStruct((t, d), F32),
        scratch_shapes=[pltpu.VMEM((2, kk, tm * ROW_SLABS, LANES), F32), pltpu.SemaphoreType.DMA((2,))],
        compiler_params=_params("arbitrary"),
        name="combine",
    )(pos, pos, x1, gates, mod, final_w, yr)


def kernel(x, c, ada_w, ada_b, norm_mix_w, w_in, ret_norm_w, lru_conv_w, lru_conv_b, lru_gate_a_w,
           lru_gate_a_b, lru_gate_x_w, lru_gate_x_b, lru_lambda, w_out, norm_ffn_w, router_w, router_b,
           moe_w_gu, moe_b_gu, moe_w_down, moe_b_down, final_norm_w):
    batch, seq, d = x.shape
    assert d == ROW_SLABS * LANES, "MoE row buffers store one (8, 128) tile per token row"
    depth = ada_w.shape[0]
    t = batch * seq
    ne = N_EXPERTS
    n_blocks = (t * TOPK_EXPERTS) // EXPERT_ROWS + ne
    tables = _retention_tables(seq)
    mod = _ada_mod(c, ada_w, ada_b).reshape(depth, batch, 1, N_MOD * d)
    w_in_bf = w_in.astype(BF16)
    w_out_bf = w_out.astype(BF16)
    wa_bd = _block_diag(lru_gate_a_w).astype(BF16)
    wx_bd = _block_diag(lru_gate_x_w).astype(BF16)
    rw_t = jnp.swapaxes(router_w, 1, 2)
    vec = lambda p: p.reshape(depth, 1, p.shape[-1])
    b_gu = moe_b_gu.reshape(depth, ne, 1, -1)
    b_dn = moe_b_down.reshape(depth, ne, 1, d)
    xf = x.reshape(t, d)
    for l in range(depth):
        ret, moba, lru = _inproj(l, xf, vec(norm_mix_w), mod, w_in_bf, seq)
        y_ret = _retention(l, ret, vec(ret_norm_w), tables, batch, seq)
        y_moba = _moba(moba, batch, seq)
        y_lru = _lru(l, lru, lru_conv_w, vec(lru_conv_b), wa_bd, vec(lru_gate_a_b), wx_bd,
                     vec(lru_gate_x_b), vec(lru_lambda), batch, seq)
        x1, h2, top_i, gates, rank, counts = _outproj_router(
            l, xf, y_ret, y_moba, y_lru, w_out_bf, mod, vec(norm_ffn_w), rw_t,
            router_b.reshape(depth, ne, 1), seq)
        pos, blk_e, meta = _plan(top_i, rank, counts, n_blocks)
        xr = _dispatch(meta, pos, h2, n_blocks * EXPERT_ROWS)
        yr = _experts(l, blk_e, meta, xr, moe_w_gu, b_gu, moe_w_down, b_dn)
        xf = _combine(l, pos, x1, gates.T, mod, final_norm_w.reshape(1, d), yr, seq,
                      final_norm=(l == depth - 1))
    return xf.reshape(batch, seq, d)
```

```python
import functools

import jax
import jax.numpy as jnp
from jax import lax
from jax.experimental import pallas as pl
from jax.experimental.pallas import tpu as pltpu

F32 = jnp.float32
BF16 = jnp.bfloat16
I32 = jnp.int32

HEAD_DIM = 64
RET_HEADS = 4
RET_W = RET_HEADS * HEAD_DIM
RET_CHUNK = 128
ROPE_BASE = 10000.0
MOBA_HEADS = 6
MOBA_W = MOBA_HEADS * HEAD_DIM
MOBA_BLOCK = 256
MOBA_TOPK = 3
LRU_BLOCKS = 6
LRU_BLOCK_W = 64
LRU_W = LRU_BLOCKS * LRU_BLOCK_W
CONV_WIDTH = 4
LRU_C = 8.0
N_EXPERTS = 32
TOPK_EXPERTS = 4
SWIGLU_LIMIT = 7.0
SWIGLU_ALPHA = 1.702
NORM_EPS = 1e-6
N_MOD = 6

LANES = 128
ROW_SLABS = 8
VMEM_LIMIT = 56 * 1024 * 1024
ROW_TILE = 512
EXPERT_ROWS = 512
MOVE_ROWS = 256
MOVE_UNROLL = 8
PLAN_ROWS = 2048
RET_SEQS = 2
LRU_GROUP = 8

NT_DIMS = (((1,), (1,)), ((), ()))
NN_DIMS = (((1,), (0,)), ((), ()))
TN_DIMS = (((0,), (0,)), ((), ()))


def _params(*semantics):
    return pltpu.CompilerParams(dimension_semantics=semantics, vmem_limit_bytes=VMEM_LIMIT)


def _dot(a, b, dims=NN_DIMS):
    return lax.dot_general(a, b, dims, preferred_element_type=F32)


def _split_bf16(a):
    hi = a.astype(BF16)
    lo = (a - hi.astype(F32)).astype(BF16)
    return hi, lo


def _dot3(a, b, dims=NN_DIMS):
    ah, al = _split_bf16(a)
    bh, bl = _split_bf16(b)
    return _dot(ah, bl, dims) + _dot(al, bh, dims) + _dot(ah, bh, dims)


def _rms_mod(x, nw, scale, shift):
    ms = jnp.mean(x * x, axis=-1, keepdims=True)
    y = x * lax.rsqrt(ms + NORM_EPS) * nw
    return y * (1.0 + scale) + shift


def _store_row_tiles(dst_ref, x):
    n, d = x.shape
    nc = d // LANES
    for c in range(nc):
        dst_ref[pl.ds(c, n, stride=nc), :] = x[:, c * LANES:(c + 1) * LANES]


def _load_row_tiles(src_ref, n, nc):
    return jnp.concatenate([src_ref[pl.ds(c, n, stride=nc), :] for c in range(nc)], axis=1)


def _mod_spec(layer, chunk, d, per_b):
    return pl.BlockSpec((1, 1, 1, d), lambda i: (layer, i // per_b, 0, chunk))


def _layer_spec(layer, shape):
    zeros = (0,) * len(shape)
    return pl.BlockSpec((1,) + tuple(shape), lambda *_: (layer,) + zeros)


def _ada_kernel(c_ref, w_ref, b_ref, o_ref):
    c = c_ref[...]
    c_act = c * jax.nn.sigmoid(c)
    o_ref[0] = _dot3(c_act, w_ref[0]) + b_ref[0]


def _ada_mod(c, ada_w, ada_b):
    depth, d, n = ada_w.shape
    b = c.shape[0]
    tn = n // 4
    return pl.pallas_call(
        _ada_kernel,
        grid=(depth, n // tn),
        in_specs=[
            pl.BlockSpec((b, d), lambda l, j: (0, 0)),
            pl.BlockSpec((1, d, tn), lambda l, j: (l, 0, j)),
            pl.BlockSpec((1, 1, tn), lambda l, j: (l, 0, j)),
        ],
        out_specs=pl.BlockSpec((1, b, tn), lambda l, j: (l, 0, j)),
        out_shape=jax.ShapeDtypeStruct((depth, b, n), F32),
        compiler_params=_params("arbitrary", "arbitrary"),
        name="ada_mod",
    )(c, ada_w, ada_b.reshape(depth, 1, n))


def _inproj_kernel(x_ref, nw_ref, sc_ref, sh_ref, w_ref, ret_ref, moba_ref, lru_ref):
    h = _rms_mod(x_ref[...], nw_ref[0], sc_ref[0, 0], sh_ref[0, 0]).astype(BF16)
    o0 = RET_W * 4
    o1 = o0 + MOBA_W * 3
    ret_ref[...] = _dot(h, w_ref[0, :, 0:o0])
    moba_ref[...] = _dot(h, w_ref[0, :, o0:o1])
    lru_ref[...] = _dot(h, w_ref[0, :, o1:])


def _inproj(layer, x2d, norm_w, mod, w_bf16, seq):
    t, d = x2d.shape
    tm = ROW_TILE
    per_b = seq // tm
    n_ret, n_moba, n_lru = RET_W * 4, MOBA_W * 3, LRU_W * 2
    return pl.pallas_call(
        _inproj_kernel,
        grid=(t // tm,),
        in_specs=[
            pl.BlockSpec((tm, d), lambda i: (i, 0)),
            _layer_spec(layer, (1, d)),
            _mod_spec(layer, 1, d, per_b),
            _mod_spec(layer, 0, d, per_b),
            _layer_spec(layer, w_bf16.shape[1:]),
        ],
        out_specs=[
            pl.BlockSpec((tm, n_ret), lambda i: (i, 0)),
            pl.BlockSpec((tm, n_moba), lambda i: (i, 0)),
            pl.BlockSpec((tm, n_lru), lambda i: (i, 0)),
        ],
        out_shape=[
            jax.ShapeDtypeStruct((t, n_ret), F32),
            jax.ShapeDtypeStruct((t, n_moba), F32),
            jax.ShapeDtypeStruct((t, n_lru), F32),
        ],
        compiler_params=_params("arbitrary"),
        name="inproj",
    )(x2d, norm_w, mod, mod, w_bf16)


def _retention_tables(seq):
    h, dh, c = RET_HEADS, HEAD_DIM, RET_CHUNK
    half = dh // 2
    inv = 1.0 / (ROPE_BASE ** (jnp.arange(half, dtype=F32) / half))
    ang = jnp.arange(seq).astype(F32)[:, None] * inv[None, :]
    cos, sin = jnp.cos(ang), jnp.sin(ang)
    cos_t = jnp.tile(jnp.concatenate([cos, cos], axis=-1), (1, h))
    sin_t = jnp.tile(jnp.concatenate([-sin, sin], axis=-1), (1, h))
    log_g = jnp.log1p(-jnp.exp2(-5.0 - jnp.arange(h, dtype=F32)))
    idx = jnp.arange(c, dtype=F32)
    diff = idx[:, None] - idx[None, :]
    dmat = jnp.where(diff >= 0, jnp.exp(log_g[:, None, None] * jnp.maximum(diff, 0.0)), 0.0)
    zeta = jnp.exp(log_g[:, None] * (c - 1 - idx)[None, :])
    xi = jnp.exp(log_g[:, None] * (idx + 1)[None, :])
    zeta_t = jnp.repeat(zeta.T, dh, axis=1)
    xi_t = jnp.repeat(xi.T, dh, axis=1)
    decay = jnp.broadcast_to(jnp.exp(log_g * c)[:, None, None], (h, dh, dh))
    return cos_t, sin_t, dmat, zeta_t, xi_t, decay


def _retention_kernel(p_ref, cos_ref, sin_ref, dmat_ref, zeta_ref, xi_ref, dec_ref, nw_ref, o_ref):
    seq = cos_ref.shape[0]
    nseq = p_ref.shape[0] // seq
    c, dh, w = RET_CHUNK, HEAD_DIM, RET_W
    lane = lax.broadcasted_iota(I32, (1, w), 1)
    first_half = (lane % dh) < (dh // 2)

    def rope(x, cos, sin):
        partner = jnp.where(first_half, pltpu.roll(x, w - dh // 2, 1), pltpu.roll(x, dh // 2, 1))
        return x * cos + partner * sin

    def chunk(n, states):
        r0 = pl.multiple_of(n * c, c)
        cos, sin = cos_ref[pl.ds(r0, c), :], sin_ref[pl.ds(r0, c), :]
        new_states = []
        for b in range(nseq):
            rows = pl.ds(pl.multiple_of(b * seq + r0, c), c)
            q = rope(p_ref[rows, 0:w], cos, sin)
            k = rope(p_ref[rows, w:2 * w], cos, sin) * (dh ** -0.5)
            v = p_ref[rows, 2 * w:3 * w]
            g = p_ref[rows, 3 * w:4 * w]
            kz = k * zeta_ref[...]
            gate = g * jax.nn.sigmoid(g) * nw_ref[0]
            for hd in range(RET_HEADS):
                state = states[b * RET_HEADS + hd]
                cols = slice(hd * dh, (hd + 1) * dh)
                qh = q[:, cols].astype(BF16)
                kh = k[:, cols].astype(BF16)
                vh = v[:, cols].astype(BF16)
                scores = _dot(qh, kh, NT_DIMS) * dmat_ref[hd]
                intra = _dot(scores.astype(BF16), vh)
                cross = _dot(qh, state.astype(BF16)) * xi_ref[:, cols]
                kv = _dot(kz[:, cols].astype(BF16), vh, TN_DIMS)
                y = intra + cross
                mu = jnp.mean(y, axis=-1, keepdims=True)
                yc = y - mu
                var = jnp.mean(yc * yc, axis=-1, keepdims=True)
                o_ref[rows, cols] = yc * lax.rsqrt(var + NORM_EPS) * gate[:, cols]
                new_states.append(state * dec_ref[hd] + kv)
        return tuple(new_states)

    init = tuple(jnp.zeros((dh, dh), F32) for _ in range(nseq * RET_HEADS))
    lax.fori_loop(0, seq // c, chunk, init)


def _retention(layer, ret, norm_w, tables, batch, seq):
    cos_t, sin_t, dmat, zeta_t, xi_t, decay = tables
    w = RET_W
    const2 = lambda b: (0, 0)
    const3 = lambda b: (0, 0, 0)
    nseq = RET_SEQS if batch % RET_SEQS == 0 else 1
    return pl.pallas_call(
        _retention_kernel,
        grid=(batch // nseq,),
        in_specs=[
            pl.BlockSpec((nseq * seq, 4 * w), lambda b: (b, 0)),
            pl.BlockSpec((seq, w), const2),
            pl.BlockSpec((seq, w), const2),
            pl.BlockSpec(dmat.shape, const3),
            pl.BlockSpec(zeta_t.shape, const2),
            pl.BlockSpec(xi_t.shape, const2),
            pl.BlockSpec(decay.shape, const3),
            _layer_spec(layer, (1, w)),
        ],
        out_specs=pl.BlockSpec((nseq * seq, w), lambda b: (b, 0)),
        out_shape=jax.ShapeDtypeStruct((batch * seq, w), F32),
        compiler_params=_params("arbitrary"),
        name="retention",
    )(ret, cos_t, sin_t, dmat, zeta_t, xi_t, decay, norm_w)


def _moba_kernel(q_ref, k_ref, v_ref, o_ref):
    seq = q_ref.shape[0]
    bk, dh = MOBA_BLOCK, HEAD_DIM
    nb = seq // bk
    scale = dh ** -0.5
    neg = -jnp.inf
    lane = lax.broadcasted_iota(I32, (1, LANES), 1)
    k_all = k_ref[...]
    k_bf = k_all.astype(BF16)
    kmean = jnp.mean(k_all.reshape(nb, bk, LANES), axis=1)
    colid = lax.broadcasted_iota(I32, (bk, nb), 1)
    causal = (lax.broadcasted_iota(I32, (bk, bk), 0)
              >= lax.broadcasted_iota(I32, (bk, bk), 1))
    heads_per_step = LANES // dh
    outs = [None] * nb
    for p in range(heads_per_step):
        head_lanes = (lane >= p * dh) & (lane < (p + 1) * dh)
        qm = jnp.where(head_lanes, q_ref[...], 0.0)
        gs = _dot3(qm, kmean, NT_DIMS)
        qm_bf = (qm * scale).astype(BF16)
        ones_lane = ((p + 1) % heads_per_step) * dh
        v_ones = jnp.where(lane == ones_lane, 1.0, v_ref[...]).astype(BF16)
        for i in range(nb):
            rows = slice(i * bk, (i + 1) * bk)
            klen = (i + 1) * bk
            s = _dot(qm_bf[rows], k_bf[:klen], NT_DIMS)
            pieces = []
            if i > MOBA_TOPK:
                gsi = gs[rows]
                rank = jnp.zeros((bk, nb), F32)
                for j2 in range(i):
                    cj = gsi[:, j2:j2 + 1]
                    tie = jnp.where(colid > j2, 1.0, 0.0)
                    rank = rank + jnp.where(cj > gsi, 1.0, jnp.where(cj == gsi, tie, 0.0))
                for j in range(i):
                    keep = rank[:, j:j + 1] < float(MOBA_TOPK)
                    pieces.append(jnp.where(keep, s[:, j * bk:(j + 1) * bk], neg))
            else:
                for j in range(i):
                    pieces.append(s[:, j * bk:(j + 1) * bk])
            pieces.append(jnp.where(causal, s[:, i * bk:klen], neg))
            m = pieces[0]
            for pc in pieces[1:]:
                m = jnp.maximum(m, pc)
            m = jnp.max(m, axis=-1, keepdims=True)
            probs = [jnp.exp(pc - m) for pc in pieces]
            pcat = probs[0] if len(probs) == 1 else jnp.concatenate(probs, axis=1)
            o = _dot(pcat.astype(BF16), v_ones[:klen])
            o = o / o[:, ones_lane:ones_lane + 1]
            outs[i] = o if outs[i] is None else jnp.where(head_lanes, o, outs[i])
    for i in range(nb):
        o_ref[i * bk:(i + 1) * bk, :] = outs[i]


def _moba(moba, batch, seq):
    npair = MOBA_W // LANES
    return pl.pallas_call(
        _moba_kernel,
        grid=(batch, npair),
        in_specs=[
            pl.BlockSpec((seq, LANES), lambda b, p: (b, p)),
            pl.BlockSpec((seq, LANES), lambda b, p: (b, npair + p)),
            pl.BlockSpec((seq, LANES), lambda b, p: (b, 2 * npair + p)),
        ],
        out_specs=pl.BlockSpec((seq, LANES), lambda b, p: (b, p)),
        out_shape=jax.ShapeDtypeStruct((batch * seq, MOBA_W), F32),
        compiler_params=_params("arbitrary", "arbitrary"),
        name="moba",
    )(moba, moba, moba)


def _lru_kernel(xg_ref, cw_ref, cb_ref, wa_ref, ba_ref, wx_ref, bx_ref, lam_ref, o_ref, a_scr, h_scr):
    seq = xg_ref.shape[0]
    w = LRU_W
    xb = xg_ref[:, 0:w]
    row = lax.broadcasted_iota(I32, (seq, w), 0)
    u = cb_ref[0] + cw_ref[0, CONV_WIDTH - 1:CONV_WIDTH, :] * xb
    for tap in range(CONV_WIDTH - 1):
        back = CONV_WIDTH - 1 - tap
        u = u + cw_ref[0, tap:tap + 1, :] * jnp.where(row >= back, pltpu.roll(xb, back, 0), 0.0)
    u_bf = u.astype(BF16)
    r = jax.nn.sigmoid(_dot(u_bf, wa_ref[0]) + ba_ref[0])
    ig = jax.nn.sigmoid(_dot(u_bf, wx_ref[0]) + bx_ref[0])
    z = -lam_ref[0]
    softplus = jnp.maximum(z, 0.0) + jnp.log1p(jnp.exp(-jnp.abs(z)))
    log_a = -LRU_C * r * softplus
    a = jnp.exp(log_a)
    b = jnp.sqrt(-jnp.tanh(log_a) * (a * a + 1.0)) * (ig * u)
    sub = row & (LRU_GROUP - 1)
    d = 1
    while d < LRU_GROUP:
        live = sub >= d
        a_prev = jnp.where(live, pltpu.roll(a, d, 0), 1.0)
        b_prev = jnp.where(live, pltpu.roll(b, d, 0), 0.0)
        b = a * b_prev + b
        a = a * a_prev
        d *= 2
    a_scr[...] = a
    h_scr[...] = b

    def group(gi, carry):
        rows = pl.ds(pl.multiple_of(gi * LRU_GROUP, LRU_GROUP), LRU_GROUP)
        h = a_scr[rows, :] * carry + h_scr[rows, :]
        h_scr[rows, :] = h
        return h[LRU_GROUP - 1:LRU_GROUP, :]

    lax.fori_loop(0, seq // LRU_GROUP, group, jnp.zeros((1, w), F32), unroll=8)
    o_ref[...] = jax.nn.gelu(xg_ref[:, w:2 * w]) * h_scr[...]


def _block_diag(wb):
    depth, g, n, _ = wb.shape
    eye = jnp.eye(g, dtype=wb.dtype)
    return (eye[None, :, None, :, None] * wb[:, :, :, None, :]).reshape(depth, g * n, g * n)


def _lru(layer, lru, cw, cb, wa_bd, ba, wx_bd, bx, lam, batch, seq):
    w = LRU_W
    vec = _layer_spec(layer, (1, w))
    mat = _layer_spec(layer, (w, w))
    return pl.pallas_call(
        _lru_kernel,
        grid=(batch,),
        in_specs=[
            pl.BlockSpec((seq, 2 * w), lambda b: (b, 0)),
            _layer_spec(layer, (CONV_WIDTH, w)),
            vec, mat, vec, mat, vec, vec,
        ],
        out_specs=pl.BlockSpec((seq, w), lambda b: (b, 0)),
        out_shape=jax.ShapeDtypeStruct((batch * seq, w), F32),
        scratch_shapes=[pltpu.VMEM((seq, w), F32), pltpu.VMEM((seq, w), F32)],
        compiler_params=_params("arbitrary"),
        name="rg_lru",
    )(lru, cw, cb, wa_bd, ba, wx_bd, bx, lam)


def _outproj_router_kernel(x_ref, yr_ref, ym_ref, yl_ref, wo_ref, g1_ref, nw_ref,
                           sc_ref, sh_ref, rw_ref, rb_ref,
                           x1_ref, h2_ref, ti_ref, tg_ref, rk_ref, cnt_ref, carry_ref):
    i = pl.program_id(0)
    tm = x_ref.shape[0]
    ne = N_EXPERTS
    o0, o1 = RET_W, RET_W + MOBA_W

    @pl.when(i == 0)
    def _():
        carry_ref[...] = jnp.zeros_like(carry_ref)

    mixed = (_dot(yr_ref[...].astype(BF16), wo_ref[0, 0:o0, :])
             + _dot(ym_ref[...].astype(BF16), wo_ref[0, o0:o1, :])
             + _dot(yl_ref[...].astype(BF16), wo_ref[0, o1:, :]))
    x1 = x_ref[...] + g1_ref[0, 0] * mixed
    x1_ref[...] = x1
    h2 = _rms_mod(x1, nw_ref[0], sc_ref[0, 0], sh_ref[0, 0])
    _store_row_tiles(h2_ref, h2)

    logits = _dot3(rw_ref[0], h2, NT_DIMS) + rb_ref[0]
    eid = lax.broadcasted_iota(I32, (ne, tm), 0)
    work = logits
    member = jnp.zeros((ne, tm), F32)
    vals, hots = [], []
    for k in range(TOPK_EXPERTS):
        m = jnp.max(work, axis=0, keepdims=True)
        idx = jnp.min(jnp.where(work == m, eid, ne), axis=0, keepdims=True)
        hot = eid == idx
        ti_ref[k:k + 1, :] = idx
        vals.append(m)
        hots.append(hot)
        member = member + jnp.where(hot, 1.0, 0.0)
        work = jnp.where(hot, -jnp.inf, work)
    exps = [jnp.exp(v - vals[0]) for v in vals]
    tot = exps[0]
    for e in exps[1:]:
        tot = tot + e
    for k in range(TOPK_EXPERTS):
        tg_ref[k:k + 1, :] = exps[k] / tot

    before = (lax.broadcasted_iota(I32, (tm, tm), 0)
              < lax.broadcasted_iota(I32, (tm, tm), 1))
    prefix = _dot(member.astype(BF16), jnp.where(before, 1.0, 0.0).astype(BF16))
    prefix = prefix + carry_ref[:, 0:1]
    for k in range(TOPK_EXPERTS):
        rk = jnp.sum(jnp.where(hots[k], prefix, 0.0), axis=0, keepdims=True)
        rk_ref[k:k + 1, :] = rk.astype(I32)
    carry_ref[...] = carry_ref[...] + jnp.sum(member, axis=1, keepdims=True)
    cnt_ref[...] = carry_ref[...].astype(I32)


def _outproj_router(layer, x2d, y_ret, y_moba, y_lru, wo_bf16, mod, norm_w, rw_t, rb, seq):
    t, d = x2d.shape
    tm = ROW_TILE
    per_b = seq // tm
    ne, kk = N_EXPERTS, TOPK_EXPERTS
    const = lambda i: (0, 0)
    row = lambda n: pl.BlockSpec((tm, n), lambda i: (i, 0))
    lanes = pl.BlockSpec((kk, tm), lambda i: (0, i))
    return pl.pallas_call(
        _outproj_router_kernel,
        grid=(t // tm,),
        in_specs=[
            row(d), row(RET_W), row(MOBA_W), row(LRU_W),
            _layer_spec(layer, wo_bf16.shape[1:]),
            _mod_spec(layer, 2, d, per_b),
            _layer_spec(layer, (1, d)),
            _mod_spec(layer, 4, d, per_b),
            _mod_spec(layer, 3, d, per_b),
            _layer_spec(layer, (ne, d)),
            _layer_spec(layer, (ne, 1)),
        ],
        out_specs=[
            row(d), pl.BlockSpec((tm * (d // LANES), LANES), lambda i: (i, 0)), lanes, lanes, lanes,
            pl.BlockSpec((ne, LANES), const),
        ],
        out_shape=[
            jax.ShapeDtypeStruct((t, d), F32),
            jax.ShapeDtypeStruct((t * (d // LANES), LANES), F32),
            jax.ShapeDtypeStruct((kk, t), I32),
            jax.ShapeDtypeStruct((kk, t), F32),
            jax.ShapeDtypeStruct((kk, t), I32),
            jax.ShapeDtypeStruct((ne, LANES), I32),
        ],
        scratch_shapes=[pltpu.VMEM((ne, LANES), F32)],
        compiler_params=_params("arbitrary"),
        name="outproj_router",
    )(x2d, y_ret, y_moba, y_lru, wo_bf16, mod, norm_w, mod, mod, rw_t, rb)


def _plan_kernel(ti_ref, rk_ref, cnt_ref, pos_ref, blk_ref, meta_ref):
    ne = N_EXPERTS
    tm = ti_ref.shape[1]
    nblk_pad = blk_ref.shape[1]
    shift = EXPERT_ROWS.bit_length() - 1
    counts = cnt_ref[...]
    padded = ((counts + (EXPERT_ROWS - 1)) >> shift) << shift
    row = lax.broadcasted_iota(I32, (ne, LANES), 0)
    pend = padded
    d = 1
    while d < ne:
        pend = pend + jnp.where(row >= d, pltpu.roll(pend, d, 0), 0)
        d *= 2
    pstart = (pend - padded)[:, 0:1]
    eid = lax.broadcasted_iota(I32, (ne, tm), 0)
    for k in range(TOPK_EXPERTS):
        base = jnp.sum(jnp.where(ti_ref[k:k + 1, :] == eid, pstart, 0), axis=0, keepdims=True)
        pos_ref[k:k + 1, :] = base + rk_ref[k:k + 1, :]
    starts = lax.broadcasted_iota(I32, (ne, nblk_pad), 1) * EXPERT_ROWS
    owner = jnp.sum(jnp.where(pend[:, 0:1] <= starts, 1, 0), axis=0, keepdims=True)
    owner = jnp.minimum(owner, ne - 1)
    blk_ref[0:1, :] = owner
    eid_blk = lax.broadcasted_iota(I32, (ne, nblk_pad), 0)
    later = (eid_blk > owner) & (padded[:, 0:1] > 0)
    blk_ref[1:2, :] = jnp.min(jnp.where(later, eid_blk, ne), axis=0, keepdims=True)
    lane = lax.broadcasted_iota(I32, (ne, LANES), 1)
    pend_lanes = jnp.sum(jnp.where(row == lane, pend, 0), axis=0, keepdims=True)
    total = jnp.max(pend, axis=0, keepdims=True)
    meta_ref[0:1, :] = pend_lanes
    meta_ref[1:2, :] = total >> shift
    meta_ref[2:3, :] = jnp.sum(jnp.where(row == lane, padded, 0), axis=0, keepdims=True)
    meta_ref[3:8, :] = jnp.zeros((5, LANES), I32)


def _plan(top_i, rank, counts, n_blocks):
    kk, t = top_i.shape
    tm = PLAN_ROWS
    nblk_pad = -(-n_blocks // LANES) * LANES
    const = lambda i: (0, 0)
    return pl.pallas_call(
        _plan_kernel,
        grid=(t // tm,),
        in_specs=[
            pl.BlockSpec((kk, tm), lambda i: (0, i)),
            pl.BlockSpec((kk, tm), lambda i: (0, i)),
            pl.BlockSpec(counts.shape, const),
        ],
        out_specs=[
            pl.BlockSpec((kk, tm), lambda i: (0, i)),
            pl.BlockSpec((2, nblk_pad), const),
            pl.BlockSpec((8, LANES), const),
        ],
        out_shape=[
            jax.ShapeDtypeStruct((kk, t), I32),
            jax.ShapeDtypeStruct((2, nblk_pad), I32),
            jax.ShapeDtypeStruct((8, LANES), I32),
        ],
        compiler_params=_params("arbitrary"),
        name="plan",
    )(top_i, rank, counts)


def _slab_rows(row, n):
    start = row * ROW_SLABS
    if not isinstance(start, int):
        start = pl.multiple_of(start, ROW_SLABS)
    return pl.ds(start, n * ROW_SLABS)


def _rows_copy(src_ref, src_row, dst_ref, dst_row, sem, n=1):
    return pltpu.make_async_copy(src_ref.at[_slab_rows(src_row, n)], dst_ref.at[_slab_rows(dst_row, n)], sem)


def _dispatch_kernel(meta_ref, pos_ref, h_ref, xr_ref, zeros_ref, hbuf, fill_sem, row_sem, load_sems):
    i = pl.program_id(0)
    last = pl.num_programs(0) - 1
    tm = pos_ref.shape[1]
    blk = EXPERT_ROWS
    n_blocks = xr_ref.shape[0] // (blk * ROW_SLABS)

    def fill(dst_row):
        return _rows_copy(zeros_ref, 0, xr_ref, dst_row, fill_sem, blk)

    def owns_rows(e):
        return meta_ref[2, e] > 0

    @pl.when(i == 0)
    def _():
        zeros_ref[...] = jnp.zeros_like(zeros_ref)
        for e in range(N_EXPERTS):
            @pl.when(owns_rows(e))
            def _():
                fill(meta_ref[0, e] - blk).start()

        def tail_start(b, carry):
            fill(b * blk).start()
            return carry

        def tail_wait(b, carry):
            fill(b * blk).wait()
            return carry

        lax.fori_loop(meta_ref[1, 0], n_blocks, tail_start, 0)
        for e in range(N_EXPERTS):
            @pl.when(owns_rows(e))
            def _():
                fill(meta_ref[0, e] - blk).wait()
        lax.fori_loop(meta_ref[1, 0], n_blocks, tail_wait, 0)

    slot = i % 2

    def load(step, dst_slot):
        return pltpu.make_async_copy(h_ref.at[_slab_rows(step * tm, tm)], hbuf.at[dst_slot],
                                     load_sems.at[dst_slot])

    def issue(g, carry):
        for u in range(MOVE_UNROLL):
            t = g * MOVE_UNROLL + u
            for k in range(TOPK_EXPERTS):
                _rows_copy(hbuf.at[slot], t, xr_ref, pos_ref[k, t], row_sem).start(priority=(u + k) % 2)
        return carry

    def drain(g, carry):
        for u in range(MOVE_UNROLL * TOPK_EXPERTS):
            _rows_copy(hbuf.at[0], 0, xr_ref, 0, row_sem).wait()
        return carry

    @pl.when(i == 0)
    def _():
        load(0, 0).start()

    @pl.when(i > 0)
    def _():
        lax.fori_loop(0, tm // MOVE_UNROLL, drain, 0)

    @pl.when(i < last)
    def _():
        load(i + 1, 1 - slot).start()

    load(i, slot).wait()
    lax.fori_loop(0, tm // MOVE_UNROLL, issue, 0)

    @pl.when(i == last)
    def _():
        lax.fori_loop(0, tm // MOVE_UNROLL, drain, 0)


def _dispatch(meta, pos, h2_tiles, n_rows):
    t = h2_tiles.shape[0] // ROW_SLABS
    tm = MOVE_ROWS
    grid_spec = pltpu.PrefetchScalarGridSpec(
        num_scalar_prefetch=1,
        grid=(t // tm,),
        in_specs=[
            pl.BlockSpec((TOPK_EXPERTS, tm), lambda i, meta: (0, i), memory_space=pltpu.SMEM),
            pl.BlockSpec(memory_space=pl.ANY),
        ],
        out_specs=pl.BlockSpec(memory_space=pl.ANY),
        scratch_shapes=[pltpu.VMEM((EXPERT_ROWS * ROW_SLABS, LANES), F32),
                        pltpu.VMEM((2, tm * ROW_SLABS, LANES), F32),
                        pltpu.SemaphoreType.DMA(()), pltpu.SemaphoreType.DMA(()),
                        pltpu.SemaphoreType.DMA((2,))],
    )
    return pl.pallas_call(
        _dispatch_kernel,
        grid_spec=grid_spec,
        out_shape=jax.ShapeDtypeStruct((n_rows * ROW_SLABS, LANES), F32),
        compiler_params=_params("arbitrary"),
        name="dispatch",
    )(meta, pos, h2_tiles)


def _expert_kernel(be_ref, meta_ref, x_ref, wgu_ref, bgu_ref, wdn_ref, bdn_ref, y_ref,
                   wgu_stage, wdn_stage, wgu_bf, wdn_bf, sems, *, layer):
    i = pl.program_id(0)
    ff = wdn_bf.shape[0]
    n_used = meta_ref[1, 0]

    def fetch(e):
        return (pltpu.make_async_copy(wgu_ref.at[layer, e], wgu_stage, sems.at[0]),
                pltpu.make_async_copy(wdn_ref.at[layer, e], wdn_stage, sems.at[1]))

    @pl.when(i >= n_used)
    def _():
        y_ref[...] = jnp.zeros_like(y_ref)

    @pl.when(i < n_used)
    def _():
        cur = be_ref[0, i]
        prev = be_ref[0, jnp.maximum(i - 1, 0)]

        @pl.when(i == 0)
        def _():
            for cp in fetch(cur):
                cp.start()

        @pl.when((i == 0) | (cur != prev))
        def _():
            for cp in fetch(cur):
                cp.wait()
            wgu_bf[...] = wgu_stage[...].astype(BF16)
            wdn_bf[...] = wdn_stage[...].astype(BF16)
            nxt = be_ref[1, i]

            @pl.when(nxt < N_EXPERTS)
            def _():
                for cp in fetch(nxt):
                    cp.start()

        x = _load_row_tiles(x_ref, EXPERT_ROWS, ROW_SLABS).astype(BF16)
        gu = _dot(x, wgu_bf[...]) + bgu_ref[0, 0]
        gate = jnp.minimum(gu[:, :ff], SWIGLU_LIMIT)
        up = jnp.clip(gu[:, ff:], -SWIGLU_LIMIT, SWIGLU_LIMIT)
        act = (up + 1.0) * gate * jax.nn.sigmoid(SWIGLU_ALPHA * gate)
        _store_row_tiles(y_ref, _dot(act.astype(BF16), wdn_bf[...]) + bdn_ref[0, 0])


def _experts(layer, blk_e, meta, xr, w_gu, b_gu, w_dn, b_dn):
    n_rows = xr.shape[0] // ROW_SLABS
    d = ROW_SLABS * LANES
    ff2 = w_gu.shape[-1]
    ff = ff2 // 2
    tm = EXPERT_ROWS
    nblk = n_rows // tm

    def last_used(i, meta):
        return jnp.minimum(i, meta[1, 0] - 1)

    xmap = lambda i, be, meta: (last_used(i, meta), 0)
    wmap = lambda i, be, meta: (layer, be[0, last_used(i, meta)], 0, 0)
    grid_spec = pltpu.PrefetchScalarGridSpec(
        num_scalar_prefetch=2,
        grid=(nblk,),
        in_specs=[
            pl.BlockSpec((tm * ROW_SLABS, LANES), xmap),
            pl.BlockSpec(memory_space=pl.ANY),
            pl.BlockSpec((1, 1, 1, ff2), wmap),
            pl.BlockSpec(memory_space=pl.ANY),
            pl.BlockSpec((1, 1, 1, d), wmap),
        ],
        out_specs=pl.BlockSpec((tm * ROW_SLABS, LANES), lambda i, be, meta: (i, 0)),
        scratch_shapes=[pltpu.VMEM((d, ff2), F32), pltpu.VMEM((ff, d), F32),
                        pltpu.VMEM((d, ff2), BF16), pltpu.VMEM((ff, d), BF16),
                        pltpu.SemaphoreType.DMA((2,))],
    )
    return pl.pallas_call(
        functools.partial(_expert_kernel, layer=layer),
        grid_spec=grid_spec,
        out_shape=jax.ShapeDtypeStruct((n_rows * ROW_SLABS, LANES), F32),
        compiler_params=_params("arbitrary"),
        name="experts",
    )(blk_e, meta, xr, w_gu, b_gu, w_dn, b_dn)


def _combine_kernel(pos_ref, nxt_ref, x1_ref, g_ref, g2_ref, fw_ref, yr_ref, o_ref, buf, sems, *, final_norm):
    i = pl.program_id(0)
    n = pl.num_programs(0)
    tm = x1_ref.shape[0]
    slot = i % 2

    def gather(table_ref, dst_slot):
        def issue(g, carry):
            for u in range(MOVE_UNROLL):
                t = g * MOVE_UNROLL + u
                for k in range(TOPK_EXPERTS):
                    _rows_copy(yr_ref, table_ref[k, t], buf.at[dst_slot, k], t,
                               sems.at[dst_slot]).start(priority=(u + k) % 2)
            return carry
        lax.fori_loop(0, tm // MOVE_UNROLL, issue, 0)

    def drain(g, carry):
        for u in range(MOVE_UNROLL * TOPK_EXPERTS):
            _rows_copy(yr_ref, 0, buf.at[slot, 0], 0, sems.at[slot]).wait()
        return carry

    @pl.when(i == 0)
    def _():
        gather(pos_ref, 0)

    @pl.when(i + 1 < n)
    def _():
        gather(nxt_ref, 1 - slot)

    lax.fori_loop(0, tm // MOVE_UNROLL, drain, 0)
    acc = g_ref[:, 0:1] * _load_row_tiles(buf.at[slot, 0], tm, ROW_SLABS)
    for k in range(1, TOPK_EXPERTS):
        acc = acc + g_ref[:, k:k + 1] * _load_row_tiles(buf.at[slot, k], tm, ROW_SLABS)
    x2 = x1_ref[...] + g2_ref[0, 0] * acc
    if final_norm:
        ms = jnp.mean(x2 * x2, axis=-1, keepdims=True)
        x2 = x2 * lax.rsqrt(ms + NORM_EPS) * fw_ref[...]
    o_ref[...] = x2


def _combine(layer, pos, x1, gates, mod, final_w, yr, seq, final_norm):
    t, d = x1.shape
    tm = MOVE_ROWS
    per_b = seq // tm
    kk = TOPK_EXPERTS
    n = t // tm
    return pl.pallas_call(
        functools.partial(_combine_kernel, final_norm=final_norm),
        grid=(n,),
        in_specs=[
            pl.BlockSpec((kk, tm), lambda i: (0, i), memory_space=pltpu.SMEM),
            pl.BlockSpec((kk, tm), lambda i: (0, jnp.minimum(i + 1, n - 1)), memory_space=pltpu.SMEM),
            pl.BlockSpec((tm, d), lambda i: (i, 0)),
            pl.BlockSpec((tm, kk), lambda i: (i, 0)),
            _mod_spec(layer, 5, d, per_b),
            pl.BlockSpec((1, d), lambda i: (0, 0)),
            pl.BlockSpec(memory_space=pl.ANY),
        ],
        out_specs=pl.BlockSpec((tm, d), lambda i: (i, 0)),
        out_shape=jax.ShapeDtypeStruct((t, d), F32),
        scratch_shapes=[pltpu.VMEM((2, kk, tm * ROW_SLABS, LANES), F32), pltpu.SemaphoreType.DMA((2,))],
        compiler_params=_params("arbitrary"),
        name="combine",
    )(pos, pos, x1, gates, mod, final_w, yr)


def kernel(x, c, ada_w, ada_b, norm_mix_w, w_in, ret_norm_w, lru_conv_w, lru_conv_b, lru_gate_a_w,
           lru_gate_a_b, lru_gate_x_w, lru_gate_x_b, lru_lambda, w_out, norm_ffn_w, router_w, router_b,
           moe_w_gu, moe_b_gu, moe_w_down, moe_b_down, final_norm_w):
    batch, seq, d = x.shape
    assert d == ROW_SLABS * LANES, "MoE row buffers store one (8, 128) tile per token row"
    depth = ada_w.shape[0]
    t = batch * seq
    ne = N_EXPERTS
    n_blocks = (t * TOPK_EXPERTS) // EXPERT_ROWS + ne
    tables = _retention_tables(seq)
    mod = _ada_mod(c, ada_w, ada_b).reshape(depth, batch, 1, N_MOD * d)
    w_in_bf = w_in.astype(BF16)
    w_out_bf = w_out.astype(BF16)
    wa_bd = _block_diag(lru_gate_a_w).astype(BF16)
    wx_bd = _block_diag(lru_gate_x_w).astype(BF16)
    rw_t = jnp.swapaxes(router_w, 1, 2)
    vec = lambda p: p.reshape(depth, 1, p.shape[-1])
    b_gu = moe_b_gu.reshape(depth, ne, 1, -1)
    b_dn = moe_b_down.reshape(depth, ne, 1, d)
    xf = x.reshape(t, d)
    for l in range(depth):
        ret, moba, lru = _inproj(l, xf, vec(norm_mix_w), mod, w_in_bf, seq)
        y_ret = _retention(l, ret, vec(ret_norm_w), tables, batch, seq)
        y_moba = _moba(moba, batch, seq)
        y_lru = _lru(l, lru, lru_conv_w, vec(lru_conv_b), wa_bd, vec(lru_gate_a_b), wx_bd,
                     vec(lru_gate_x_b), vec(lru_lambda), batch, seq)
        x1, h2, top_i, gates, rank, counts = _outproj_router(
            l, xf, y_ret, y_moba, y_lru, w_out_bf, mod, vec(norm_ffn_w), rw_t,
            router_b.reshape(depth, ne, 1), seq)
        pos, blk_e, meta = _plan(top_i, rank, counts, n_blocks)
        xr = _dispatch(meta, pos, h2, n_blocks * EXPERT_ROWS)
        yr = _experts(l, blk_e, meta, xr, moe_w_gu, b_gu, moe_w_down, b_dn)
        xf = _combine(l, pos, x1, gates.T, mod, final_norm_w.reshape(1, d), yr, seq,
                      final_norm=(l == depth - 1))
    return xf.reshape(batch, seq, d)
```

```python
import functools

import jax
import jax.numpy as jnp
from jax import lax
from jax.experimental import pallas as pl
from jax.experimental.pallas import tpu as pltpu

F32 = jnp.float32
BF16 = jnp.bfloat16
I32 = jnp.int32

HEAD_DIM = 64
RET_HEADS = 4
RET_W = RET_HEADS * HEAD_DIM
RET_CHUNK = 128
ROPE_BASE = 10000.0
MOBA_HEADS = 6
MOBA_W = MOBA_HEADS * HEAD_DIM
MOBA_BLOCK = 256
MOBA_TOPK = 3
LRU_BLOCKS = 6
LRU_BLOCK_W = 64
LRU_W = LRU_BLOCKS * LRU_BLOCK_W
CONV_WIDTH = 4
LRU_C = 8.0
N_EXPERTS = 32
TOPK_EXPERTS = 4
SWIGLU_LIMIT = 7.0
SWIGLU_ALPHA = 1.702
NORM_EPS = 1e-6
N_MOD = 6

LANES = 128
ROW_SLABS = 8
VMEM_LIMIT = 56 * 1024 * 1024
ROW_TILE = 512
EXPERT_ROWS = 512
MOVE_ROWS = 256
MOVE_UNROLL = 8
PLAN_ROWS = 2048
RET_SEQS = 2
LRU_GROUP = 8

NT_DIMS = (((1,), (1,)), ((), ()))
NN_DIMS = (((1,), (0,)), ((), ()))
TN_DIMS = (((0,), (0,)), ((), ()))


def _params(*semantics):
    return pltpu.CompilerParams(dimension_semantics=semantics, vmem_limit_bytes=VMEM_LIMIT)


def _dot(a, b, dims=NN_DIMS):
    return lax.dot_general(a, b, dims, preferred_element_type=F32)


def _split_bf16(a):
    hi = a.astype(BF16)
    lo = (a - hi.astype(F32)).astype(BF16)
    return hi, lo


def _dot3(a, b, dims=NN_DIMS):
    ah, al = _split_bf16(a)
    bh, bl = _split_bf16(b)
    return _dot(ah, bl, dims) + _dot(al, bh, dims) + _dot(ah, bh, dims)


def _rms_mod(x, nw, scale, shift):
    ms = jnp.mean(x * x, axis=-1, keepdims=True)
    y = x * lax.rsqrt(ms + NORM_EPS) * nw
    return y * (1.0 + scale) + shift


def _store_row_tiles(dst_ref, x):
    n, d = x.shape
    nc = d // LANES
    for c in range(nc):
        dst_ref[pl.ds(c, n, stride=nc), :] = x[:, c * LANES:(c + 1) * LANES]


def _load_row_tiles(src_ref, n, nc):
    return jnp.concatenate([src_ref[pl.ds(c, n, stride=nc), :] for c in range(nc)], axis=1)


def _mod_spec(layer, chunk, d, per_b):
    return pl.BlockSpec((1, 1, 1, d), lambda i: (layer, i // per_b, 0, chunk))


def _layer_spec(layer, shape):
    zeros = (0,) * len(shape)
    return pl.BlockSpec((1,) + tuple(shape), lambda *_: (layer,) + zeros)


def _ada_kernel(c_ref, w_ref, b_ref, o_ref):
    c = c_ref[...]
    c_act = c * jax.nn.sigmoid(c)
    o_ref[0] = _dot3(c_act, w_ref[0]) + b_ref[0]


def _ada_mod(c, ada_w, ada_b):
    depth, d, n = ada_w.shape
    b = c.shape[0]
    tn = n // 4
    return pl.pallas_call(
        _ada_kernel,
        grid=(depth, n // tn),
        in_specs=[
            pl.BlockSpec((b, d), lambda l, j: (0, 0)),
            pl.BlockSpec((1, d, tn), lambda l, j: (l, 0, j)),
            pl.BlockSpec((1, 1, tn), lambda l, j: (l, 0, j)),
        ],
        out_specs=pl.BlockSpec((1, b, tn), lambda l, j: (l, 0, j)),
        out_shape=jax.ShapeDtypeStruct((depth, b, n), F32),
        compiler_params=_params("arbitrary", "arbitrary"),
        name="ada_mod",
    )(c, ada_w, ada_b.reshape(depth, 1, n))


def _inproj_kernel(x_ref, nw_ref, sc_ref, sh_ref, w_ref, ret_ref, moba_ref, lru_ref):
    h = _rms_mod(x_ref[...], nw_ref[0], sc_ref[0, 0], sh_ref[0, 0]).astype(BF16)
    o0 = RET_W * 4
    o1 = o0 + MOBA_W * 3
    ret_ref[...] = _dot(h, w_ref[0, :, 0:o0])
    moba_ref[...] = _dot(h, w_ref[0, :, o0:o1])
    lru_ref[...] = _dot(h, w_ref[0, :, o1:])


def _inproj(layer, x2d, norm_w, mod, w_bf16, seq):
    t, d = x2d.shape
    tm = ROW_TILE
    per_b = seq // tm
    n_ret, n_moba, n_lru = RET_W * 4, MOBA_W * 3, LRU_W * 2
    return pl.pallas_call(
        _inproj_kernel,
        grid=(t // tm,),
        in_specs=[
            pl.BlockSpec((tm, d), lambda i: (i, 0)),
            _layer_spec(layer, (1, d)),
            _mod_spec(layer, 1, d, per_b),
            _mod_spec(layer, 0, d, per_b),
            _layer_spec(layer, w_bf16.shape[1:]),
        ],
        out_specs=[
            pl.BlockSpec((tm, n_ret), lambda i: (i, 0)),
            pl.BlockSpec((tm, n_moba), lambda i: (i, 0)),
            pl.BlockSpec((tm, n_lru), lambda i: (i, 0)),
        ],
        out_shape=[
            jax.ShapeDtypeStruct((t, n_ret), F32),
            jax.ShapeDtypeStruct((t, n_moba), F32),
            jax.ShapeDtypeStruct((t, n_lru), F32),
        ],
        compiler_params=_params("arbitrary"),
        name="inproj",
    )(x2d, norm_w, mod, mod, w_bf16)


def _retention_tables(seq):
    h, dh, c = RET_HEADS, HEAD_DIM, RET_CHUNK
    half = dh // 2
    inv = 1.0 / (ROPE_BASE ** (jnp.arange(half, dtype=F32) / half))
    ang = jnp.arange(seq).astype(F32)[:, None] * inv[None, :]
    cos, sin = jnp.cos(ang), jnp.sin(ang)
    cos_t = jnp.tile(jnp.concatenate([cos, cos], axis=-1), (1, h))
    sin_t = jnp.tile(jnp.concatenate([-sin, sin], axis=-1), (1, h))
    log_g = jnp.log1p(-jnp.exp2(-5.0 - jnp.arange(h, dtype=F32)))
    idx = jnp.arange(c, dtype=F32)
    diff = idx[:, None] - idx[None, :]
    dmat = jnp.where(diff >= 0, jnp.exp(log_g[:, None, None] * jnp.maximum(diff, 0.0)), 0.0)
    zeta = jnp.exp(log_g[:, None] * (c - 1 - idx)[None, :])
    xi = jnp.exp(log_g[:, None] * (idx + 1)[None, :])
    zeta_t = jnp.repeat(zeta.T, dh, axis=1)
    xi_t = jnp.repeat(xi.T, dh, axis=1)
    decay = jnp.broadcast_to(jnp.exp(log_g * c)[:, None, None], (h, dh, dh))
    return cos_t, sin_t, dmat, zeta_t, xi_t, decay


def _retention_kernel(p_ref, cos_ref, sin_ref, dmat_ref, zeta_ref, xi_ref, dec_ref, nw_ref, o_ref):
    seq = cos_ref.shape[0]
    nseq = p_ref.shape[0] // seq
    c, dh, w = RET_CHUNK, HEAD_DIM, RET_W
    lane = lax.broadcasted_iota(I32, (1, w), 1)
    first_half = (lane % dh) < (dh // 2)

    def rope(x, cos, sin):
        partner = jnp.where(first_half, pltpu.roll(x, w - dh // 2, 1), pltpu.roll(x, dh // 2, 1))
        return x * cos + partner * sin

    def chunk(n, states):
        r0 = pl.multiple_of(n * c, c)
        cos, sin = cos_ref[pl.ds(r0, c), :], sin_ref[pl.ds(r0, c), :]
        new_states = []
        for b in range(nseq):
            rows = pl.ds(pl.multiple_of(b * seq + r0, c), c)
            q = rope(p_ref[rows, 0:w], cos, sin)
            k = rope(p_ref[rows, w:2 * w], cos, sin) * (dh ** -0.5)
            v = p_ref[rows, 2 * w:3 * w]
            g = p_ref[rows, 3 * w:4 * w]
            kz = k * zeta_ref[...]
            gate = g * jax.nn.sigmoid(g) * nw_ref[0]
            for hd in range(RET_HEADS):
                state = states[b * RET_HEADS + hd]
                cols = slice(hd * dh, (hd + 1) * dh)
                qh = q[:, cols].astype(BF16)
                kh = k[:, cols].astype(BF16)
                vh = v[:, cols].astype(BF16)
                scores = _dot(qh, kh, NT_DIMS) * dmat_ref[hd]
                intra = _dot(scores.astype(BF16), vh)
                cross = _dot(qh, state.astype(BF16)) * xi_ref[:, cols]
                kv = _dot(kz[:, cols].astype(BF16), vh, TN_DIMS)
                y = intra + cross
                mu = jnp.mean(y, axis=-1, keepdims=True)
                yc = y - mu
                var = jnp.mean(yc * yc, axis=-1, keepdims=True)
                o_ref[rows, cols] = yc * lax.rsqrt(var + NORM_EPS) * gate[:, cols]
                new_states.append(state * dec_ref[hd] + kv)
        return tuple(new_states)

    init = tuple(jnp.zeros((dh, dh), F32) for _ in range(nseq * RET_HEADS))
    lax.fori_loop(0, seq // c, chunk, init)


def _retention(layer, ret, norm_w, tables, batch, seq):
    cos_t, sin_t, dmat, zeta_t, xi_t, decay = tables
    w = RET_W
    const2 = lambda b: (0, 0)
    const3 = lambda b: (0, 0, 0)
    nseq = RET_SEQS if batch % RET_SEQS == 0 else 1
    return pl.pallas_call(
        _retention_kernel,
        grid=(batch // nseq,),
        in_specs=[
            pl.BlockSpec((nseq * seq, 4 * w), lambda b: (b, 0)),
            pl.BlockSpec((seq, w), const2),
            pl.BlockSpec((seq, w), const2),
            pl.BlockSpec(dmat.shape, const3),
            pl.BlockSpec(zeta_t.shape, const2),
            pl.BlockSpec(xi_t.shape, const2),
            pl.BlockSpec(decay.shape, const3),
            _layer_spec(layer, (1, w)),
        ],
        out_specs=pl.BlockSpec((nseq * seq, w), lambda b: (b, 0)),
        out_shape=jax.ShapeDtypeStruct((batch * seq, w), F32),
        compiler_params=_params("arbitrary"),
        name="retention",
    )(ret, cos_t, sin_t, dmat, zeta_t, xi_t, decay, norm_w)


def _moba_kernel(q_ref, k_ref, v_ref, o_ref):
    seq = q_ref.shape[0]
    bk, dh = MOBA_BLOCK, HEAD_DIM
    nb = seq // bk
    scale = dh ** -0.5
    neg = -jnp.inf
    lane = lax.broadcasted_iota(I32, (1, LANES), 1)
    k_all = k_ref[...]
    k_bf = k_all.astype(BF16)
    v_bf = v_ref[...].astype(BF16)
    kmean = jnp.mean(k_all.reshape(nb, bk, LANES), axis=1)
    colid = lax.broadcasted_iota(I32, (bk, nb), 1)
    causal = (lax.broadcasted_iota(I32, (bk, bk), 0)
              >= lax.broadcasted_iota(I32, (bk, bk), 1))
    heads_per_step = LANES // dh
    outs = [None] * nb
    for p in range(heads_per_step):
        head_lanes = (lane >= p * dh) & (lane < (p + 1) * dh)
        qm = jnp.where(head_lanes, q_ref[...], 0.0)
        gs = _dot3(qm, kmean, NT_DIMS)
        qm_bf = (qm * scale).astype(BF16)
        for i in range(nb):
            rows = slice(i * bk, (i + 1) * bk)
            klen = (i + 1) * bk
            s = _dot(qm_bf[rows], k_bf[:klen], NT_DIMS)
            pieces = []
            if i > MOBA_TOPK:
                gsi = gs[rows]
                rank = jnp.zeros((bk, nb), F32)
                for j2 in range(i):
                    cj = gsi[:, j2:j2 + 1]
                    tie = jnp.where(colid > j2, 1.0, 0.0)
                    rank = rank + jnp.where(cj > gsi, 1.0, jnp.where(cj == gsi, tie, 0.0))
                for j in range(i):
                    keep = rank[:, j:j + 1] < float(MOBA_TOPK)
                    pieces.append(jnp.where(keep, s[:, j * bk:(j + 1) * bk], neg))
            else:
                for j in range(i):
                    pieces.append(s[:, j * bk:(j + 1) * bk])
            pieces.append(jnp.where(causal, s[:, i * bk:klen], neg))
            m = pieces[0]
            for pc in pieces[1:]:
                m = jnp.maximum(m, pc)
            m = jnp.max(m, axis=-1, keepdims=True)
            probs = [jnp.exp(pc - m) for pc in pieces]
            tot = probs[0]
            for pr in probs[1:]:
                tot = tot + pr
            denom = jnp.sum(tot, axis=-1, keepdims=True)
            pcat = probs[0] if len(probs) == 1 else jnp.concatenate(probs, axis=1)
            o = _dot(pcat.astype(BF16), v_bf[:klen]) / denom
            outs[i] = o if outs[i] is None else jnp.where(head_lanes, o, outs[i])
    for i in range(nb):
        o_ref[i * bk:(i + 1) * bk, :] = outs[i]


def _moba(moba, batch, seq):
    npair = MOBA_W // LANES
    return pl.pallas_call(
        _moba_kernel,
        grid=(batch, npair),
        in_specs=[
            pl.BlockSpec((seq, LANES), lambda b, p: (b, p)),
            pl.BlockSpec((seq, LANES), lambda b, p: (b, npair + p)),
            pl.BlockSpec((seq, LANES), lambda b, p: (b, 2 * npair + p)),
        ],
        out_specs=pl.BlockSpec((seq, LANES), lambda b, p: (b, p)),
        out_shape=jax.ShapeDtypeStruct((batch * seq, MOBA_W), F32),
        compiler_params=_params("arbitrary", "arbitrary"),
        name="moba",
    )(moba, moba, moba)


def _lru_kernel(xg_ref, cw_ref, cb_ref, wa_ref, ba_ref, wx_ref, bx_ref, lam_ref, o_ref, a_scr, h_scr):
    seq = xg_ref.shape[0]
    w = LRU_W
    xb = xg_ref[:, 0:w]
    row = lax.broadcasted_iota(I32, (seq, w), 0)
    u = cb_ref[0] + cw_ref[0, CONV_WIDTH - 1:CONV_WIDTH, :] * xb
    for tap in range(CONV_WIDTH - 1):
        back = CONV_WIDTH - 1 - tap
        u = u + cw_ref[0, tap:tap + 1, :] * jnp.where(row >= back, pltpu.roll(xb, back, 0), 0.0)
    u_bf = u.astype(BF16)
    r = jax.nn.sigmoid(_dot(u_bf, wa_ref[0]) + ba_ref[0])
    ig = jax.nn.sigmoid(_dot(u_bf, wx_ref[0]) + bx_ref[0])
    z = -lam_ref[0]
    softplus = jnp.maximum(z, 0.0) + jnp.log1p(jnp.exp(-jnp.abs(z)))
    log_a = -LRU_C * r * softplus
    a = jnp.exp(log_a)
    b = jnp.sqrt(-jnp.tanh(log_a) * (a * a + 1.0)) * (ig * u)
    sub = row & (LRU_GROUP - 1)
    d = 1
    while d < LRU_GROUP:
        live = sub >= d
        a_prev = jnp.where(live, pltpu.roll(a, d, 0), 1.0)
        b_prev = jnp.where(live, pltpu.roll(b, d, 0), 0.0)
        b = a * b_prev + b
        a = a * a_prev
        d *= 2
    a_scr[...] = a
    h_scr[...] = b

    def group(gi, carry):
        rows = pl.ds(pl.multiple_of(gi * LRU_GROUP, LRU_GROUP), LRU_GROUP)
        h = a_scr[rows, :] * carry + h_scr[rows, :]
        h_scr[rows, :] = h
        return h[LRU_GROUP - 1:LRU_GROUP, :]

    lax.fori_loop(0, seq // LRU_GROUP, group, jnp.zeros((1, w), F32), unroll=8)
    o_ref[...] = jax.nn.gelu(xg_ref[:, w:2 * w]) * h_scr[...]


def _block_diag(wb):
    depth, g, n, _ = wb.shape
    eye = jnp.eye(g, dtype=wb.dtype)
    return (eye[None, :, None, :, None] * wb[:, :, :, None, :]).reshape(depth, g * n, g * n)


def _lru(layer, lru, cw, cb, wa_bd, ba, wx_bd, bx, lam, batch, seq):
    w = LRU_W
    vec = _layer_spec(layer, (1, w))
    mat = _layer_spec(layer, (w, w))
    return pl.pallas_call(
        _lru_kernel,
        grid=(batch,),
        in_specs=[
            pl.BlockSpec((seq, 2 * w), lambda b: (b, 0)),
            _layer_spec(layer, (CONV_WIDTH, w)),
            vec, mat, vec, mat, vec, vec,
        ],
        out_specs=pl.BlockSpec((seq, w), lambda b: (b, 0)),
        out_shape=jax.ShapeDtypeStruct((batch * seq, w), F32),
        scratch_shapes=[pltpu.VMEM((seq, w), F32), pltpu.VMEM((seq, w), F32)],
        compiler_params=_params("arbitrary"),
        name="rg_lru",
    )(lru, cw, cb, wa_bd, ba, wx_bd, bx, lam)


def _outproj_router_kernel(x_ref, yr_ref, ym_ref, yl_ref, wo_ref, g1_ref, nw_ref,
                           sc_ref, sh_ref, rw_ref, rb_ref,
                           x1_ref, h2_ref, ti_ref, tg_ref, rk_ref, cnt_ref, carry_ref):
    i = pl.program_id(0)
    tm = x_ref.shape[0]
    ne = N_EXPERTS
    o0, o1 = RET_W, RET_W + MOBA_W

    @pl.when(i == 0)
    def _():
        carry_ref[...] = jnp.zeros_like(carry_ref)

    mixed = (_dot(yr_ref[...].astype(BF16), wo_ref[0, 0:o0, :])
             + _dot(ym_ref[...].astype(BF16), wo_ref[0, o0:o1, :])
             + _dot(yl_ref[...].astype(BF16), wo_ref[0, o1:, :]))
    x1 = x_ref[...] + g1_ref[0, 0] * mixed
    x1_ref[...] = x1
    h2 = _rms_mod(x1, nw_ref[0], sc_ref[0, 0], sh_ref[0, 0])
    _store_row_tiles(h2_ref, h2)

    logits = jnp.transpose(_dot3(h2, rw_ref[0]))[0:ne, :] + rb_ref[0]
    eid = lax.broadcasted_iota(I32, (ne, tm), 0)
    work = logits
    member = jnp.zeros((ne, tm), F32)
    vals, hots = [], []
    for k in range(TOPK_EXPERTS):
        m = jnp.max(work, axis=0, keepdims=True)
        idx = jnp.min(jnp.where(work == m, eid, ne), axis=0, keepdims=True)
        hot = eid == idx
        ti_ref[k:k + 1, :] = idx
        vals.append(m)
        hots.append(hot)
        member = member + jnp.where(hot, 1.0, 0.0)
        work = jnp.where(hot, -jnp.inf, work)
    exps = [jnp.exp(v - vals[0]) for v in vals]
    tot = exps[0]
    for e in exps[1:]:
        tot = tot + e
    for k in range(TOPK_EXPERTS):
        tg_ref[k:k + 1, :] = exps[k] / tot

    before = (lax.broadcasted_iota(I32, (tm, tm), 0)
              < lax.broadcasted_iota(I32, (tm, tm), 1))
    prefix = _dot(member.astype(BF16), jnp.where(before, 1.0, 0.0).astype(BF16))
    prefix = prefix + carry_ref[:, 0:1]
    for k in range(TOPK_EXPERTS):
        rk = jnp.sum(jnp.where(hots[k], prefix, 0.0), axis=0, keepdims=True)
        rk_ref[k:k + 1, :] = rk.astype(I32)
    carry_ref[...] = carry_ref[...] + jnp.sum(member, axis=1, keepdims=True)
    cnt_ref[...] = carry_ref[...].astype(I32)


def _outproj_router(layer, x2d, y_ret, y_moba, y_lru, wo_bf16, mod, norm_w, rw_t, rb, seq):
    t, d = x2d.shape
    tm = ROW_TILE
    per_b = seq // tm
    ne, kk = N_EXPERTS, TOPK_EXPERTS
    const = lambda i: (0, 0)
    row = lambda n: pl.BlockSpec((tm, n), lambda i: (i, 0))
    lanes = pl.BlockSpec((kk, tm), lambda i: (0, i))
    return pl.pallas_call(
        _outproj_router_kernel,
        grid=(t // tm,),
        in_specs=[
            row(d), row(RET_W), row(MOBA_W), row(LRU_W),
            _layer_spec(layer, wo_bf16.shape[1:]),
            _mod_spec(layer, 2, d, per_b),
            _layer_spec(layer, (1, d)),
            _mod_spec(layer, 4, d, per_b),
            _mod_spec(layer, 3, d, per_b),
            _layer_spec(layer, (d, LANES)),
            _layer_spec(layer, (ne, 1)),
        ],
        out_specs=[
            row(d), pl.BlockSpec((tm * (d // LANES), LANES), lambda i: (i, 0)), lanes, lanes, lanes,
            pl.BlockSpec((ne, LANES), const),
        ],
        out_shape=[
            jax.ShapeDtypeStruct((t, d), F32),
            jax.ShapeDtypeStruct((t * (d // LANES), LANES), F32),
            jax.ShapeDtypeStruct((kk, t), I32),
            jax.ShapeDtypeStruct((kk, t), F32),
            jax.ShapeDtypeStruct((kk, t), I32),
            jax.ShapeDtypeStruct((ne, LANES), I32),
        ],
        scratch_shapes=[pltpu.VMEM((ne, LANES), F32)],
        compiler_params=_params("arbitrary"),
        name="outproj_router",
    )(x2d, y_ret, y_moba, y_lru, wo_bf16, mod, norm_w, mod, mod, rw_t, rb)


def _plan_kernel(ti_ref, rk_ref, cnt_ref, pos_ref, blk_ref, meta_ref):
    ne = N_EXPERTS
    tm = ti_ref.shape[1]
    nblk_pad = blk_ref.shape[1]
    shift = EXPERT_ROWS.bit_length() - 1
    counts = cnt_ref[...]
    padded = ((counts + (EXPERT_ROWS - 1)) >> shift) << shift
    row = lax.broadcasted_iota(I32, (ne, LANES), 0)
    pend = padded
    d = 1
    while d < ne:
        pend = pend + jnp.where(row >= d, pltpu.roll(pend, d, 0), 0)
        d *= 2
    pstart = (pend - padded)[:, 0:1]
    eid = lax.broadcasted_iota(I32, (ne, tm), 0)
    for k in range(TOPK_EXPERTS):
        base = jnp.sum(jnp.where(ti_ref[k:k + 1, :] == eid, pstart, 0), axis=0, keepdims=True)
        pos_ref[k:k + 1, :] = base + rk_ref[k:k + 1, :]
    starts = lax.broadcasted_iota(I32, (ne, nblk_pad), 1) * EXPERT_ROWS
    owner = jnp.sum(jnp.where(pend[:, 0:1] <= starts, 1, 0), axis=0, keepdims=True)
    owner = jnp.minimum(owner, ne - 1)
    blk_ref[0:1, :] = owner
    eid_blk = lax.broadcasted_iota(I32, (ne, nblk_pad), 0)
    later = (eid_blk > owner) & (padded[:, 0:1] > 0)
    blk_ref[1:2, :] = jnp.min(jnp.where(later, eid_blk, ne), axis=0, keepdims=True)
    lane = lax.broadcasted_iota(I32, (ne, LANES), 1)
    pend_lanes = jnp.sum(jnp.where(row == lane, pend, 0), axis=0, keepdims=True)
    total = jnp.max(pend, axis=0, keepdims=True)
    meta_ref[0:1, :] = pend_lanes
    meta_ref[1:2, :] = total >> shift
    meta_ref[2:3, :] = jnp.sum(jnp.where(row == lane, padded, 0), axis=0, keepdims=True)
    meta_ref[3:8, :] = jnp.zeros((5, LANES), I32)


def _plan(top_i, rank, counts, n_blocks):
    kk, t = top_i.shape
    tm = PLAN_ROWS
    nblk_pad = -(-n_blocks // LANES) * LANES
    const = lambda i: (0, 0)
    return pl.pallas_call(
        _plan_kernel,
        grid=(t // tm,),
        in_specs=[
            pl.BlockSpec((kk, tm), lambda i: (0, i)),
            pl.BlockSpec((kk, tm), lambda i: (0, i)),
            pl.BlockSpec(counts.shape, const),
        ],
        out_specs=[
            pl.BlockSpec((kk, tm), lambda i: (0, i)),
            pl.BlockSpec((2, nblk_pad), const),
            pl.BlockSpec((8, LANES), const),
        ],
        out_shape=[
            jax.ShapeDtypeStruct((kk, t), I32),
            jax.ShapeDtypeStruct((2, nblk_pad), I32),
            jax.ShapeDtypeStruct((8, LANES), I32),
        ],
        compiler_params=_params("arbitrary"),
        name="plan",
    )(top_i, rank, counts)


def _slab_rows(row, n):
    start = row * ROW_SLABS
    if not isinstance(start, int):
        start = pl.multiple_of(start, ROW_SLABS)
    return pl.ds(start, n * ROW_SLABS)


def _rows_copy(src_ref, src_row, dst_ref, dst_row, sem, n=1):
    return pltpu.make_async_copy(src_ref.at[_slab_rows(src_row, n)], dst_ref.at[_slab_rows(dst_row, n)], sem)


def _dispatch_kernel(meta_ref, pos_ref, h_ref, xr_ref, zeros_ref, hbuf, fill_sem, row_sem, load_sems):
    i = pl.program_id(0)
    last = pl.num_programs(0) - 1
    tm = pos_ref.shape[1]
    blk = EXPERT_ROWS
    n_blocks = xr_ref.shape[0] // (blk * ROW_SLABS)

    def fill(dst_row):
        return _rows_copy(zeros_ref, 0, xr_ref, dst_row, fill_sem, blk)

    def owns_rows(e):
        return meta_ref[2, e] > 0

    @pl.when(i == 0)
    def _():
        zeros_ref[...] = jnp.zeros_like(zeros_ref)
        for e in range(N_EXPERTS):
            @pl.when(owns_rows(e))
            def _():
                fill(meta_ref[0, e] - blk).start()

        def tail_start(b, carry):
            fill(b * blk).start()
            return carry

        def tail_wait(b, carry):
            fill(b * blk).wait()
            return carry

        lax.fori_loop(meta_ref[1, 0], n_blocks, tail_start, 0)
        for e in range(N_EXPERTS):
            @pl.when(owns_rows(e))
            def _():
                fill(meta_ref[0, e] - blk).wait()
        lax.fori_loop(meta_ref[1, 0], n_blocks, tail_wait, 0)

    slot = i % 2

    def load(step, dst_slot):
        return pltpu.make_async_copy(h_ref.at[_slab_rows(step * tm, tm)], hbuf.at[dst_slot],
                                     load_sems.at[dst_slot])

    def issue(g, carry):
        for u in range(MOVE_UNROLL):
            t = g * MOVE_UNROLL + u
            for k in range(TOPK_EXPERTS):
                _rows_copy(hbuf.at[slot], t, xr_ref, pos_ref[k, t], row_sem).start(priority=(u + k) % 2)
        return carry

    def drain(g, carry):
        for u in range(MOVE_UNROLL * TOPK_EXPERTS):
            _rows_copy(hbuf.at[0], 0, xr_ref, 0, row_sem).wait()
        return carry

    @pl.when(i == 0)
    def _():
        load(0, 0).start()

    @pl.when(i > 0)
    def _():
        lax.fori_loop(0, tm // MOVE_UNROLL, drain, 0)

    @pl.when(i < last)
    def _():
        load(i + 1, 1 - slot).start()

    load(i, slot).wait()
    lax.fori_loop(0, tm // MOVE_UNROLL, issue, 0)

    @pl.when(i == last)
    def _():
        lax.fori_loop(0, tm // MOVE_UNROLL, drain, 0)


def _dispatch(meta, pos, h2_tiles, n_rows):
    t = h2_tiles.shape[0] // ROW_SLABS
    tm = MOVE_ROWS
    grid_spec = pltpu.PrefetchScalarGridSpec(
        num_scalar_prefetch=1,
        grid=(t // tm,),
        in_specs=[
            pl.BlockSpec((TOPK_EXPERTS, tm), lambda i, meta: (0, i), memory_space=pltpu.SMEM),
            pl.BlockSpec(memory_space=pl.ANY),
        ],
        out_specs=pl.BlockSpec(memory_space=pl.ANY),
        scratch_shapes=[pltpu.VMEM((EXPERT_ROWS * ROW_SLABS, LANES), F32),
                        pltpu.VMEM((2, tm * ROW_SLABS, LANES), F32),
                        pltpu.SemaphoreType.DMA(()), pltpu.SemaphoreType.DMA(()),
                        pltpu.SemaphoreType.DMA((2,))],
    )
    return pl.pallas_call(
        _dispatch_kernel,
        grid_spec=grid_spec,
        out_shape=jax.ShapeDtypeStruct((n_rows * ROW_SLABS, LANES), F32),
        compiler_params=_params("arbitrary"),
        name="dispatch",
    )(meta, pos, h2_tiles)


def _expert_kernel(be_ref, meta_ref, x_ref, wgu_ref, bgu_ref, wdn_ref, bdn_ref, y_ref,
                   wgu_stage, wdn_stage, wgu_bf, wdn_bf, sems, *, layer):
    i = pl.program_id(0)
    ff = wdn_bf.shape[0]
    n_used = meta_ref[1, 0]

    def fetch(e):
        return (pltpu.make_async_copy(wgu_ref.at[layer, e], wgu_stage, sems.at[0]),
                pltpu.make_async_copy(wdn_ref.at[layer, e], wdn_stage, sems.at[1]))

    @pl.when(i >= n_used)
    def _():
        y_ref[...] = jnp.zeros_like(y_ref)

    @pl.when(i < n_used)
    def _():
        cur = be_ref[0, i]
        prev = be_ref[0, jnp.maximum(i - 1, 0)]

        @pl.when(i == 0)
        def _():
            for cp in fetch(cur):
                cp.start()

        @pl.when((i == 0) | (cur != prev))
        def _():
            for cp in fetch(cur):
                cp.wait()
            wgu_bf[...] = wgu_stage[...].astype(BF16)
            wdn_bf[...] = wdn_stage[...].astype(BF16)
            nxt = be_ref[1, i]

            @pl.when(nxt < N_EXPERTS)
            def _():
                for cp in fetch(nxt):
                    cp.start()

        x = _load_row_tiles(x_ref, EXPERT_ROWS, ROW_SLABS).astype(BF16)
        gu = _dot(x, wgu_bf[...]) + bgu_ref[0, 0]
        gate = jnp.minimum(gu[:, :ff], SWIGLU_LIMIT)
        up = jnp.clip(gu[:, ff:], -SWIGLU_LIMIT, SWIGLU_LIMIT)
        act = (up + 1.0) * gate * jax.nn.sigmoid(SWIGLU_ALPHA * gate)
        _store_row_tiles(y_ref, _dot(act.astype(BF16), wdn_bf[...]) + bdn_ref[0, 0])


def _experts(layer, blk_e, meta, xr, w_gu, b_gu, w_dn, b_dn):
    n_rows = xr.shape[0] // ROW_SLABS
    d = ROW_SLABS * LANES
    ff2 = w_gu.shape[-1]
    ff = ff2 // 2
    tm = EXPERT_ROWS
    nblk = n_rows // tm

    def last_used(i, meta):
        return jnp.minimum(i, meta[1, 0] - 1)

    xmap = lambda i, be, meta: (last_used(i, meta), 0)
    wmap = lambda i, be, meta: (layer, be[0, last_used(i, meta)], 0, 0)
    grid_spec = pltpu.PrefetchScalarGridSpec(
        num_scalar_prefetch=2,
        grid=(nblk,),
        in_specs=[
            pl.BlockSpec((tm * ROW_SLABS, LANES), xmap),
            pl.BlockSpec(memory_space=pl.ANY),
            pl.BlockSpec((1, 1, 1, ff2), wmap),
            pl.BlockSpec(memory_space=pl.ANY),
            pl.BlockSpec((1, 1, 1, d), wmap),
        ],
        out_specs=pl.BlockSpec((tm * ROW_SLABS, LANES), lambda i, be, meta: (i, 0)),
        scratch_shapes=[pltpu.VMEM((d, ff2), F32), pltpu.VMEM((ff, d), F32),
                        pltpu.VMEM((d, ff2), BF16), pltpu.VMEM((ff, d), BF16),
                        pltpu.SemaphoreType.DMA((2,))],
    )
    return pl.pallas_call(
        functools.partial(_expert_kernel, layer=layer),
        grid_spec=grid_spec,
        out_shape=jax.ShapeDtypeStruct((n_rows * ROW_SLABS, LANES), F32),
        compiler_params=_params("arbitrary"),
        name="experts",
    )(blk_e, meta, xr, w_gu, b_gu, w_dn, b_dn)


def _combine_kernel(pos_ref, nxt_ref, x1_ref, g_ref, g2_ref, fw_ref, yr_ref, o_ref, buf, sems, *, final_norm):
    i = pl.program_id(0)
    n = pl.num_programs(0)
    tm = x1_ref.shape[0]
    slot = i % 2

    def gather(table_ref, dst_slot):
        def issue(g, carry):
            for u in range(MOVE_UNROLL):
                t = g * MOVE_UNROLL + u
                for k in range(TOPK_EXPERTS):
                    _rows_copy(yr_ref, table_ref[k, t], buf.at[dst_slot, k], t,
                               sems.at[dst_slot]).start(priority=(u + k) % 2)
            return carry
        lax.fori_loop(0, tm // MOVE_UNROLL, issue, 0)

    def drain(g, carry):
        for u in range(MOVE_UNROLL * TOPK_EXPERTS):
            _rows_copy(yr_ref, 0, buf.at[slot, 0], 0, sems.at[slot]).wait()
        return carry

    @pl.when(i == 0)
    def _():
        gather(pos_ref, 0)

    @pl.when(i + 1 < n)
    def _():
        gather(nxt_ref, 1 - slot)

    lax.fori_loop(0, tm // MOVE_UNROLL, drain, 0)
    acc = g_ref[:, 0:1] * _load_row_tiles(buf.at[slot, 0], tm, ROW_SLABS)
    for k in range(1, TOPK_EXPERTS):
        acc = acc + g_ref[:, k:k + 1] * _load_row_tiles(buf.at[slot, k], tm, ROW_SLABS)
    x2 = x1_ref[...] + g2_ref[0, 0] * acc
    if final_norm:
        ms = jnp.mean(x2 * x2, axis=-1, keepdims=True)
        x2 = x2 * lax.rsqrt(ms + NORM_EPS) * fw_ref[...]
    o_ref[...] = x2


def _combine(layer, pos, x1, gates, mod, final_w, yr, seq, final_norm):
    t, d = x1.shape
    tm = MOVE_ROWS
    per_b = seq // tm
    kk = TOPK_EXPERTS
    n = t // tm
    return pl.pallas_call(
        functools.partial(_combine_kernel, final_norm=final_norm),
        grid=(n,),
        in_specs=[
            pl.BlockSpec((kk, tm), lambda i: (0, i), memory_space=pltpu.SMEM),
            pl.BlockSpec((kk, tm), lambda i: (0, jnp.minimum(i + 1, n - 1)), memory_space=pltpu.SMEM),
            pl.BlockSpec((tm, d), lambda i: (i, 0)),
            pl.BlockSpec((tm, kk), lambda i: (i, 0)),
            _mod_spec(layer, 5, d, per_b),
            pl.BlockSpec((1, d), lambda i: (0, 0)),
            pl.BlockSpec(memory_space=pl.ANY),
        ],
        out_specs=pl.BlockSpec((tm, d), lambda i: (i, 0)),
        out_shape=jax.ShapeDtypeStruct((t, d), F32),
        scratch_shapes=[pltpu.VMEM((2, kk, tm * ROW_SLABS, LANES), F32), pltpu.SemaphoreType.DMA((2,))],
        compiler_params=_params("arbitrary"),
        name="combine",
    )(pos, pos, x1, gates, mod, final_w, yr)


def kernel(x, c, ada_w, ada_b, norm_mix_w, w_in, ret_norm_w, lru_conv_w, lru_conv_b, lru_gate_a_w,
           lru_gate_a_b, lru_gate_x_w, lru_gate_x_b, lru_lambda, w_out, norm_ffn_w, router_w, router_b,
           moe_w_gu, moe_b_gu, moe_w_down, moe_b_down, final_norm_w):
    batch, seq, d = x.shape
    assert d == ROW_SLABS * LANES, "MoE row buffers store one (8, 128) tile per token row"
    depth = ada_w.shape[0]
    t = batch * seq
    ne = N_EXPERTS
    n_blocks = (t * TOPK_EXPERTS) // EXPERT_ROWS + ne
    tables = _retention_tables(seq)
    mod = _ada_mod(c, ada_w, ada_b).reshape(depth, batch, 1, N_MOD * d)
    w_in_bf = w_in.astype(BF16)
    w_out_bf = w_out.astype(BF16)
    wa_bd = _block_diag(lru_gate_a_w).astype(BF16)
    wx_bd = _block_diag(lru_gate_x_w).astype(BF16)
    rw_t = jnp.pad(router_w, ((0, 0), (0, 0), (0, LANES - ne)))
    vec = lambda p: p.reshape(depth, 1, p.shape[-1])
    b_gu = moe_b_gu.reshape(depth, ne, 1, -1)
    b_dn = moe_b_down.reshape(depth, ne, 1, d)
    xf = x.reshape(t, d)
    for l in range(depth):
        ret, moba, lru = _inproj(l, xf, vec(norm_mix_w), mod, w_in_bf, seq)
        y_ret = _retention(l, ret, vec(ret_norm_w), tables, batch, seq)
        y_moba = _moba(moba, batch, seq)
        y_lru = _lru(l, lru, lru_conv_w, vec(lru_conv_b), wa_bd, vec(lru_gate_a_b), wx_bd,
                     vec(lru_gate_x_b), vec(lru_lambda), batch, seq)
        x1, h2, top_i, gates, rank, counts = _outproj_router(
            l, xf, y_ret, y_moba, y_lru, w_out_bf, mod, vec(norm_ffn_w), rw_t,
            router_b.reshape(depth, ne, 1), seq)
        pos, blk_e, meta = _plan(top_i, rank, counts, n_blocks)
        xr = _dispatch(meta, pos, h2, n_blocks * EXPERT_ROWS)
        yr = _experts(l, blk_e, meta, xr, moe_w_gu, b_gu, moe_w_down, b_dn)
        xf = _combine(l, pos, x1, gates.T, mod, final_norm_w.reshape(1, d), yr, seq,
                      final_norm=(l == depth - 1))
    return xf.reshape(batch, seq, d)
```
